```python
import math
import jax, jax.numpy as jnp
from jax import lax
import numpy as np

D_MODEL = 1024
BATCH = 2
SEQ = 8192
DEPTH = 4
DEC_BATCH = 128
DEC_SEQ = 8
PAST_LEN = 8192
PAGE_SIZE = 128

N_META = 16
POOL_WINDOWS = (2, 4, 8, 16)
POOL_GROUPS = 4
POOL_WIDTH = D_MODEL // 2
POOL_GW = POOL_WIDTH // POOL_GROUPS
POOL_OUT_GW = D_MODEL // POOL_GROUPS
POOL_HIST = 15
CONV_DIM = D_MODEL // 2
CONV_WIDTH = 31
CONV_HIST = CONV_WIDTH - 1
HEAD_DIM = 64
N_HEADS = D_MODEL // HEAD_DIM
N_KV_HEADS = 4
GQA_GROUP = N_HEADS // N_KV_HEADS
WINDOW = 128
ATTN_BLOCK = 128
N_BUCKETS = 32
MAX_DISTANCE = 128
D_FF = 4 * D_MODEL
N_BRANCHES = 3
NORM_EPS = 1e-6
D_IN = POOL_WIDTH + 2 * CONV_DIM + (N_HEADS + 2 * N_KV_HEADS) * HEAD_DIM + N_BRANCHES * D_MODEL

kernel_name = 'hybrid_pool_conformer_swa_gated_decoder_step'


def _rmsnorm(x, g):
    x32 = x.astype(jnp.float32)
    y = x32 * lax.rsqrt(jnp.mean(x32 * x32, axis=-1, keepdims=True) + NORM_EPS)
    return (y * g.astype(jnp.float32)).astype(x.dtype)


def _t5_bucket(dist):
    n = np.maximum(dist, 0)
    exact = N_BUCKETS // 2
    large = exact + (np.log(np.maximum(n, 1) / exact) / np.log(MAX_DISTANCE / exact)
                     * (N_BUCKETS - exact)).astype(np.int32)
    large = np.minimum(large, N_BUCKETS - 1)
    return np.where(n < exact, n, large).astype(np.int32)


def _pool_mixer(a, past, start, w, scale):
    B, T, _ = a.shape
    ext = jnp.concatenate([past.astype(a.dtype), a], axis=1)
    csum = jnp.pad(jnp.cumsum(ext.astype(jnp.float32), axis=1), ((0, 0), (1, 0), (0, 0)))
    pos = start + jnp.arange(T)
    a32 = a.astype(jnp.float32)
    parts = []
    for gi, win in enumerate(POOL_WINDOWS):
        c0, c1 = gi * POOL_GW, (gi + 1) * POOL_GW
        hi = csum[:, POOL_HIST + 1:POOL_HIST + 1 + T, c0:c1]
        lo = csum[:, POOL_HIST + 1 - win:POOL_HIST + 1 - win + T, c0:c1]
        cnt = jnp.minimum(pos + 1, win).astype(jnp.float32)[None, :, None]
        parts.append((hi - lo) / cnt - a32[..., c0:c1])
    r = jnp.stack(parts, axis=2).astype(a.dtype)
    y = jnp.einsum('btgc,gcd->btgd', r, w).reshape(B, T, D_MODEL) * scale
    return y, ext[:, -POOL_HIST:]


def _conv_mixer(c_in, past, w, b, ng, nb, w_out):
    glu = c_in[..., :CONV_DIM] * jax.nn.sigmoid(c_in[..., CONV_DIM:])
    ext = jnp.concatenate([past.astype(glu.dtype), glu], axis=1)
    y = lax.conv_general_dilated(ext, w[:, None, :].astype(ext.dtype), (1,), 'VALID',
                                 dimension_numbers=('NWC', 'WIO', 'NWC'),
                                 feature_group_count=CONV_DIM) + b.astype(ext.dtype)
    y32 = y.astype(jnp.float32)
    mu = jnp.mean(y32, axis=-1, keepdims=True)
    var = jnp.mean(jnp.square(y32 - mu), axis=-1, keepdims=True)
    yn = (y32 - mu) * lax.rsqrt(var + NORM_EPS) * ng.astype(jnp.float32) + nb.astype(jnp.float32)
    s = jax.nn.silu(yn).astype(glu.dtype)
    return s @ w_out, ext[:, -CONV_HIST:]


def _attn_core(qb, kb, vb, dist, key_valid, sinks, rel_bias):
    B, N, Q = qb.shape[:3]
    S = kb.shape[2]
    mask = ((dist >= 0) & (dist <= WINDOW))[None]
    if key_valid is not None:
        mask = mask & key_valid[:, None, :]
    bias = rel_bias[_t5_bucket(dist)].astype(jnp.float32)
    bias = bias.transpose(2, 0, 1).reshape(N_KV_HEADS, GQA_GROUP, Q, S)
    q = qb.reshape(B, N, Q, N_KV_HEADS, GQA_GROUP, HEAD_DIM)
    logits = jnp.einsum('bnqkgd,bnskd->bnkgqs', q, kb,
                        preferred_element_type=jnp.float32) * (HEAD_DIM ** -0.5) + bias
    logits = jnp.where(mask[:, None, None], logits, -jnp.inf)
    sink = sinks.astype(jnp.float32).reshape(N_KV_HEADS, GQA_GROUP, 1)
    m = jnp.maximum(logits.max(axis=-1), sink)
    p = jnp.exp(logits - m[..., None])
    denom = p.sum(axis=-1) + jnp.exp(sink - m)
    out = jnp.einsum('bnkgqs,bnskd->bnqkgd', (p / denom[..., None]).astype(vb.dtype), vb)
    return out.reshape(B, N, Q, N_HEADS * HEAD_DIM)


def _attention_prompt(q, k, v, sinks, rel_bias):
    B, T = q.shape[:2]
    pad = (-T) % ATTN_BLOCK
    nb = (T + pad) // ATTN_BLOCK
    padf = lambda x: jnp.pad(x, ((0, 0), (pad, 0), (0, 0), (0, 0)))
    qb = padf(q).reshape(B, nb, ATTN_BLOCK, N_HEADS, HEAD_DIM)
    kc = padf(k).reshape(B, nb, ATTN_BLOCK, N_KV_HEADS, HEAD_DIM)
    vc = padf(v).reshape(B, nb, ATTN_BLOCK, N_KV_HEADS, HEAD_DIM)
    band = lambda x: jnp.concatenate(
        [jnp.concatenate([jnp.zeros_like(x[:, :1]), x[:, :-1]], axis=1), x], axis=2)
    kb, vb = band(kc), band(vc)
    dist = ATTN_BLOCK + np.arange(ATTN_BLOCK)[:, None] - np.arange(2 * ATTN_BLOCK)[None, :]
    key_idx = (np.arange(nb)[:, None] - 1) * ATTN_BLOCK + np.arange(2 * ATTN_BLOCK)[None, :]
    key_valid = key_idx >= pad
    out = _attn_core(qb, kb, vb, dist, key_valid, sinks, rel_bias)
    out = out.reshape(B, nb * ATTN_BLOCK, N_HEADS * HEAD_DIM)[:, pad:]
    return out, k[:, -WINDOW:], v[:, -WINDOW:]


def _attention_sample(q, k, v, past_k, past_v, sinks, rel_bias):
    T = q.shape[1]
    W = past_k.shape[1]
    kk = jnp.concatenate([past_k.astype(k.dtype), k], axis=1)
    vv = jnp.concatenate([past_v.astype(v.dtype), v], axis=1)
    dist = W + np.arange(T)[:, None] - np.arange(W + T)[None, :]
    out = _attn_core(q[:, None], kk[:, None], vv[:, None], dist, None, sinks, rel_bias)[:, 0]
    return out, kk[:, -WINDOW:], vv[:, -WINDOW:]


def _layer(h, start, pool_past, conv_past, kv_past, rel_bias, norm_mix, w_in, b_in, pool_w,
           pool_scale, conv_w, conv_b, conv_norm_g, conv_norm_b, w_conv_out, attn_sinks,
           w_attn_out, w_out, norm_mlp, w_up, w_down):
    B, T, _ = h.shape
    u = _rmsnorm(h, norm_mix)
    z = u @ w_in + b_in
    sizes = [POOL_WIDTH, 2 * CONV_DIM, N_HEADS * HEAD_DIM, N_KV_HEADS * HEAD_DIM,
             N_KV_HEADS * HEAD_DIM, N_BRANCHES * D_MODEL]
    a_in, c_in, q, k, v, g = jnp.split(z, np.cumsum(sizes)[:-1].tolist(), axis=-1)
    y_a, pool_state = _pool_mixer(a_in, pool_past, start, pool_w, pool_scale)
    y_b, conv_state = _conv_mixer(c_in, conv_past, conv_w, conv_b, conv_norm_g, conv_norm_b,
                                  w_conv_out)
    q = q.reshape(B, T, N_HEADS, HEAD_DIM)
    k = k.reshape(B, T, N_KV_HEADS, HEAD_DIM)
    v = v.reshape(B, T, N_KV_HEADS, HEAD_DIM)
    if kv_past is None:
        o, k_state, v_state = _attention_prompt(q, k, v, attn_sinks, rel_bias)
    else:
        o, k_state, v_state = _attention_sample(q, k, v, kv_past[0], kv_past[1], attn_sinks,
                                                rel_bias)
    y_c = o @ w_attn_out
    gates = jax.nn.sigmoid(g.astype(jnp.float32)).reshape(B, T, N_BRANCHES, D_MODEL)
    merged = gates[:, :, 0] * y_a + gates[:, :, 1] * y_b + gates[:, :, 2] * y_c
    h = h + merged.astype(h.dtype) @ w_out
    u2 = _rmsnorm(h, norm_mlp)
    h = h + jnp.square(jax.nn.relu(u2 @ w_up)) @ w_down
    return h, pool_state, conv_state, k_state, v_state


def setup_inputs(seed: int = 0) -> dict:
    key = jax.random.key(seed)
    ks = jax.random.split(key, 25)
    nrm = lambda k, shape, s: jax.random.normal(k, shape, jnp.float32) * s
    return {
        'x_prompt': nrm(ks[0], (BATCH, SEQ, D_MODEL), 1.0),
        'x_sample': nrm(ks[1], (DEC_BATCH, DEC_SEQ, D_MODEL), 1.0),
        'state_pool': nrm(ks[2], (DEPTH, DEC_BATCH, POOL_HIST, POOL_WIDTH), 1.0),
        'state_conv': nrm(ks[3], (DEPTH, DEC_BATCH, CONV_HIST, CONV_DIM), 0.5),
        'cache_k': nrm(ks[4], (DEPTH, DEC_BATCH, WINDOW, N_KV_HEADS, HEAD_DIM), 1.0),
        'cache_v': nrm(ks[5], (DEPTH, DEC_BATCH, WINDOW, N_KV_HEADS, HEAD_DIM), 1.0),
        'meta_tokens': nrm(ks[6], (N_META, D_MODEL), 1.0),
        'rel_bias': nrm(ks[7], (N_BUCKETS, N_HEADS), 0.5),
        'norm_mix': 1.0 + nrm(ks[8], (DEPTH, D_MODEL), 0.02),
        'w_in': nrm(ks[9], (DEPTH, D_MODEL, D_IN), D_MODEL ** -0.5),
        'b_in': nrm(ks[10], (DEPTH, D_IN), 0.01),
        'pool_w': nrm(ks[11], (DEPTH, POOL_GROUPS, POOL_GW, POOL_OUT_GW), POOL_GW ** -0.5),
        'pool_scale': 1.0 + nrm(ks[12], (DEPTH, D_MODEL), 0.02),
        'conv_w': nrm(ks[13], (DEPTH, CONV_WIDTH, CONV_DIM), CONV_WIDTH ** -0.5),
        'conv_b': nrm(ks[14], (DEPTH, CONV_DIM), 0.01),
        'conv_norm_g': 1.0 + nrm(ks[15], (DEPTH, CONV_DIM), 0.02),
        'conv_norm_b': nrm(ks[16], (DEPTH, CONV_DIM), 0.01),
        'w_conv_out': nrm(ks[17], (DEPTH, CONV_DIM, D_MODEL), CONV_DIM ** -0.5),
        'attn_sinks': nrm(ks[18], (DEPTH, N_HEADS), 1.0),
        'w_attn_out': nrm(ks[19], (DEPTH, N_HEADS * HEAD_DIM, D_MODEL), (N_HEADS * HEAD_DIM) ** -0.5),
        'w_out': nrm(ks[20], (DEPTH, D_MODEL, D_MODEL), D_MODEL ** -0.5),
        'norm_mlp': 1.0 + nrm(ks[21], (DEPTH, D_MODEL), 0.02),
        'w_up': nrm(ks[22], (DEPTH, D_MODEL, D_FF), D_MODEL ** -0.5),
        'w_down': nrm(ks[23], (DEPTH, D_FF, D_MODEL), D_FF ** -0.5),
        'norm_final': 1.0 + nrm(ks[24], (D_MODEL,), 0.02),
    }


def reference(x_prompt, x_sample, state_pool, state_conv, cache_k, cache_v, meta_tokens,
              rel_bias, norm_mix, w_in, b_in, pool_w, pool_scale, conv_w, conv_b, conv_norm_g,
              conv_norm_b, w_conv_out, attn_sinks, w_attn_out, w_out, norm_mlp, w_up, w_down,
              norm_final):
    bp = x_prompt.shape[0]
    meta = jnp.broadcast_to(meta_tokens[None].astype(x_prompt.dtype), (bp, N_META, D_MODEL))
    hp = jnp.concatenate([meta, x_prompt], axis=1)
    hs = x_sample
    zero_pool = jnp.zeros((bp, POOL_HIST, POOL_WIDTH), x_prompt.dtype)
    zero_conv = jnp.zeros((bp, CONV_HIST, CONV_DIM), x_prompt.dtype)
    pool_p, pool_s, conv_p, conv_s, k_p, k_s, v_p, v_s = [], [], [], [], [], [], [], []
    for l in range(DEPTH):
        wl = (norm_mix[l], w_in[l], b_in[l], pool_w[l], pool_scale[l], conv_w[l], conv_b[l],
              conv_norm_g[l], conv_norm_b[l], w_conv_out[l], attn_sinks[l], w_attn_out[l],
              w_out[l], norm_mlp[l], w_up[l], w_down[l])
        hp, ps, cs, kst, vst = _layer(hp, 0, zero_pool, zero_conv, None, rel_bias, *wl)
        pool_p.append(ps); conv_p.append(cs); k_p.append(kst); v_p.append(vst)
        hs, ps, cs, kst, vst = _layer(hs, PAST_LEN, state_pool[l], state_conv[l],
                                      (cache_k[l], cache_v[l]), rel_bias, *wl)
        pool_s.append(ps); conv_s.append(cs); k_s.append(kst); v_s.append(vst)
    y_prompt = _rmsnorm(hp, norm_final)[:, N_META:]
    y_sample = _rmsnorm(hs, norm_final)
    return (y_prompt, y_sample, jnp.stack(pool_p), jnp.stack(pool_s), jnp.stack(conv_p),
            jnp.stack(conv_s), jnp.stack(k_p), jnp.stack(k_s), jnp.stack(v_p), jnp.stack(v_s))
```

```python
import functools

import numpy as np
import jax
import jax.numpy as jnp
from jax import lax
from jax.experimental import pallas as pl
from jax.experimental.pallas import tpu as pltpu

F32 = jnp.float32
BF16 = jnp.bfloat16

D_MODEL = 1024
N_META = 16
POOL_WINDOWS = (2, 4, 8, 16)
POOL_WIDTH = 512
POOL_GW = 128
POOL_OUT_GW = 256
POOL_HIST = 15
CONV_DIM = 512
CONV_WIDTH = 31
CONV_HIST = 30
HEAD_DIM = 64
N_HEADS = 16
N_KV_HEADS = 4
GQA_GROUP = 4
KV_DIM = N_KV_HEADS * HEAD_DIM
WINDOW = 128
N_BUCKETS = 32
MAX_DISTANCE = 128
D_FF = 4096
NORM_EPS = 1e-6
C_POOL = 0
C_GLU = C_POOL + POOL_WIDTH
C_Q = C_GLU + 2 * CONV_DIM
C_K = C_Q + N_HEADS * HEAD_DIM
C_V = C_K + KV_DIM
C_GATE = C_V + KV_DIM
D_IN = C_GATE + 3 * D_MODEL

SUBLANES = 8
LANES = 128
POOL_CARRY = 16
CONV_CARRY = 32
FF_CHUNK = 1024
CONV_ROWS = 64
VMEM_LIMIT = 60 * 1024 * 1024


def _dot(a, b):
    return jnp.dot(a, b, preferred_element_type=F32)


def _dot_nt(a, b):
    return lax.dot_general(a, b, (((1,), (1,)), ((), ())), preferred_element_type=F32)


def _rms(x, g):
    ms = jnp.mean(x * x, axis=-1, keepdims=True)
    return x * lax.rsqrt(ms + NORM_EPS) * g


def _layernorm_silu(y, g, b):
    mu = jnp.mean(y, axis=-1, keepdims=True)
    yc = y - mu
    var = jnp.mean(yc * yc, axis=-1, keepdims=True)
    yn = yc * lax.rsqrt(var + NORM_EPS) * g + b
    return yn * jax.nn.sigmoid(yn)


def _softmax_with_sink(logits, sink):
    m = jnp.maximum(jnp.max(logits, axis=-1, keepdims=True), sink)
    p = jnp.exp(logits - m)
    denom = jnp.sum(p, axis=-1, keepdims=True) + jnp.exp(sink - m)
    return p * (1.0 / denom)


def _const_spec(block_shape, index):
    return pl.BlockSpec(block_shape, lambda *_: index, pipeline_mode=pl.Buffered(1))


def _t5_bucket(dist):
    n = np.maximum(dist, 0)
    exact = N_BUCKETS // 2
    large = exact + (np.log(np.maximum(n, 1) / exact) / np.log(MAX_DISTANCE / exact)
                     * (N_BUCKETS - exact)).astype(np.int32)
    large = np.minimum(large, N_BUCKETS - 1)
    return np.where(n < exact, n, large).astype(np.int32)


def _banded_bias(rel_bias, dist, valid):
    b = rel_bias[_t5_bucket(dist)].astype(F32).transpose(2, 0, 1)
    mask = (dist >= 0) & (dist <= WINDOW) & valid
    return jnp.where(jnp.asarray(mask)[None], b, -jnp.inf)


def _prompt_mixer_body(sinks_ref, h_ref, nm_ref, win_ref, bin_ref, pw_ref, ps_ref, cw_ref,
                       cb_ref, cng_ref, cnb_ref, wco_ref, bias_ref, wao_ref, wo_ref,
                       ho_ref, pst_ref, cst_ref, kst_ref, vst_ref,
                       u_scr, exta, extc, kbuf, vbuf, q_scr, o_scr, s_scr, mrg,
                       *, layer, tm, last_tile, e_loc):
    t = pl.program_id(1)

    @pl.when(t == 0)
    def _():
        exta[0:POOL_CARRY, :] = jnp.zeros((POOL_CARRY, POOL_WIDTH), F32)
        extc[0:CONV_CARRY, :] = jnp.zeros((CONV_CARRY, CONV_DIM), F32)
        kbuf[0:WINDOW, :] = jnp.zeros((WINDOW, KV_DIM), BF16)
        vbuf[0:WINDOW, :] = jnp.zeros((WINDOW, KV_DIM), BF16)

    @pl.when(t > 0)
    def _():
        exta[0:POOL_CARRY, :] = exta[tm:tm + POOL_CARRY, :]
        extc[0:CONV_CARRY, :] = extc[tm:tm + CONV_CARRY, :]
        kbuf[0:WINDOW, :] = kbuf[tm:tm + WINDOW, :]
        vbuf[0:WINDOW, :] = vbuf[tm:tm + WINDOW, :]

    x = h_ref[...]
    u_scr[...] = _rms(x, nm_ref[...]).astype(BF16)

    def proj(c0, width):
        return _dot(u_scr[...], win_ref[:, c0:c0 + width]) + bin_ref[:, c0:c0 + width]

    def gate(i):
        return jax.nn.sigmoid(proj(C_GATE + i * D_MODEL, D_MODEL))

    a = proj(C_POOL, POOL_WIDTH)
    exta[POOL_CARRY:POOL_CARRY + tm, :] = a
    pos = t * tm + lax.broadcasted_iota(jnp.int32, (tm, 1), 0)
    for gi, win in enumerate(POOL_WINDOWS):
        c0 = gi * POOL_GW
        own = a[:, c0:c0 + POOL_GW]
        s = own
        for j in range(1, win):
            s = s + exta[POOL_CARRY - j:POOL_CARRY - j + tm, c0:c0 + POOL_GW]
        cnt = jnp.minimum(pos + 1, win).astype(F32)
        r = (s / cnt - own).astype(BF16)
        o0 = gi * POOL_OUT_GW
        ya = _dot(r, pw_ref[gi]) * ps_ref[:, o0:o0 + POOL_OUT_GW]
        mrg[:, o0:o0 + POOL_OUT_GW] = ya
    mrg[...] = gate(0) * mrg[...]

    c = proj(C_GLU, 2 * CONV_DIM)
    extc[CONV_CARRY:CONV_CARRY + tm, :] = c[:, :CONV_DIM] * jax.nn.sigmoid(c[:, CONV_DIM:])
    first = CONV_CARRY - CONV_HIST
    for r0 in range(0, tm, CONV_ROWS):
        acc = jnp.zeros((CONV_ROWS, CONV_DIM), F32)
        for j in range(CONV_WIDTH):
            acc = acc + extc[first + r0 + j:first + r0 + j + CONV_ROWS, :] * cw_ref[j:j + 1, :]
        y = acc + cb_ref[...]
        s_scr[r0:r0 + CONV_ROWS, :] = _layernorm_silu(y, cng_ref[...], cnb_ref[...]).astype(BF16)
    yb = _dot(s_scr[...], wco_ref[...])
    mrg[...] = mrg[...] + gate(1) * yb

    q_scr[...] = (proj(C_Q, N_HEADS * HEAD_DIM) * (HEAD_DIM ** -0.5)).astype(BF16)
    kv = proj(C_K, 2 * KV_DIM)
    kbuf[WINDOW:WINDOW + tm, :] = kv[:, :KV_DIM].astype(BF16)
    vbuf[WINDOW:WINDOW + tm, :] = kv[:, KV_DIM:].astype(BF16)

    @pl.when(t == last_tile)
    def _():
        pst_ref[...] = exta[e_loc:e_loc + POOL_CARRY, :]
        cst_ref[...] = extc[e_loc:e_loc + CONV_CARRY, :]
        kst_ref[...] = kv[e_loc - WINDOW:e_loc, :KV_DIM]
        vst_ref[...] = kv[e_loc - WINDOW:e_loc, KV_DIM:]

    for qb in range(tm // WINDOW):
        r0 = qb * WINDOW
        sel = jnp.where(t == 0, 1, 0) if qb == 0 else 0
        for g_kv in range(N_KV_HEADS):
            heads = [g_kv * GQA_GROUP + g for g in range(GQA_GROUP)]
            q4 = jnp.concatenate(
                [q_scr[r0:r0 + WINDOW, h * HEAD_DIM:(h + 1) * HEAD_DIM] for h in heads], axis=0)
            kw = kbuf[r0:r0 + 2 * WINDOW, g_kv * HEAD_DIM:(g_kv + 1) * HEAD_DIM]
            vw = vbuf[r0:r0 + 2 * WINDOW, g_kv * HEAD_DIM:(g_kv + 1) * HEAD_DIM]
            logits = _dot_nt(q4, kw) + bias_ref[sel, g_kv]
            sink = jnp.concatenate(
                [jnp.full((WINDOW, 1), sinks_ref[layer, h], F32) for h in heads], axis=0)
            p = _softmax_with_sink(logits, sink).astype(BF16)
            o4 = _dot(p, vw)
            for g, h in enumerate(heads):
                o_scr[r0:r0 + WINDOW, h * HEAD_DIM:(h + 1) * HEAD_DIM] = (
                    o4[g * WINDOW:(g + 1) * WINDOW, :].astype(BF16))
    yc = _dot(o_scr[...], wao_ref[...])
    merged = mrg[...] + gate(2) * yc
    ho_ref[...] = x + _dot(merged.astype(BF16), wo_ref[...])


def _prompt_mixer(layer, hp, t_valid, tm, sinks, norm_mix, w_in, b_in, pool_w, pool_scale,
                  conv_w, conv_b, conv_ng, conv_nb, w_conv_out, bias_tab, w_attn_out, w_out):
    nb, tp, _ = hp.shape
    n_tiles = tp // tm
    last_tile = (t_valid - 1) // tm
    e_loc = t_valid - last_tile * tm
    assert last_tile == n_tiles - 1 and e_loc >= WINDOW and e_loc % SUBLANES == 0
    body = functools.partial(_prompt_mixer_body, layer=layer, tm=tm, last_tile=last_tile,
                             e_loc=e_loc)
    lsel3 = (layer, 0, 0)
    in_specs = [
        pl.BlockSpec(memory_space=pltpu.SMEM),
        pl.BlockSpec((None, tm, D_MODEL), lambda b, t: (b, t, 0)),
        _const_spec((None, 1, D_MODEL), lsel3),
        _const_spec((None, D_MODEL, D_IN), lsel3),
        _const_spec((None, 1, D_IN), lsel3),
        _const_spec((None, len(POOL_WINDOWS), POOL_GW, POOL_OUT_GW), (layer, 0, 0, 0)),
        _const_spec((None, 1, D_MODEL), lsel3),
        _const_spec((None, CONV_WIDTH, CONV_DIM), lsel3),
        _const_spec((None, 1, CONV_DIM), lsel3),
        _const_spec((None, 1, CONV_DIM), lsel3),
        _const_spec((None, 1, CONV_DIM), lsel3),
        _const_spec((None, CONV_DIM, D_MODEL), lsel3),
        _const_spec((2, N_KV_HEADS, GQA_GROUP * WINDOW, 2 * WINDOW), (0, 0, 0, 0)),
        _const_spec((None, D_MODEL, D_MODEL), lsel3),
        _const_spec((None, D_MODEL, D_MODEL), lsel3),
    ]
    out_shape = [
        jax.ShapeDtypeStruct((nb, tp, D_MODEL), F32),
        jax.ShapeDtypeStruct((nb, POOL_CARRY, POOL_WIDTH), F32),
        jax.ShapeDtypeStruct((nb, CONV_CARRY, CONV_DIM), F32),
        jax.ShapeDtypeStruct((nb, WINDOW, KV_DIM), F32),
        jax.ShapeDtypeStruct((nb, WINDOW, KV_DIM), F32),
    ]
    out_specs = [
        pl.BlockSpec((None, tm, D_MODEL), lambda b, t: (b, t, 0)),
        pl.BlockSpec((None, POOL_CARRY, POOL_WIDTH), lambda b, t: (b, 0, 0)),
        pl.BlockSpec((None, CONV_CARRY, CONV_DIM), lambda b, t: (b, 0, 0)),
        pl.BlockSpec((None, WINDOW, KV_DIM), lambda b, t: (b, 0, 0)),
        pl.BlockSpec((None, WINDOW, KV_DIM), lambda b, t: (b, 0, 0)),
    ]
    scratch = [
        pltpu.VMEM((tm, D_MODEL), BF16),
        pltpu.VMEM((POOL_CARRY + tm, POOL_WIDTH), F32),
        pltpu.VMEM((CONV_CARRY + tm, CONV_DIM), F32),
        pltpu.VMEM((WINDOW + tm, KV_DIM), BF16),
        pltpu.VMEM((WINDOW + tm, KV_DIM), BF16),
        pltpu.VMEM((tm, D_MODEL), BF16),
        pltpu.VMEM((tm, D_MODEL), BF16),
        pltpu.VMEM((tm, CONV_DIM), BF16),
        pltpu.VMEM((tm, D_MODEL), F32),
    ]
    return pl.pallas_call(
        body,
        grid=(nb, n_tiles),
        in_specs=in_specs,
        out_specs=out_specs,
        out_shape=out_shape,
        scratch_shapes=scratch,
        compiler_params=pltpu.CompilerParams(
            dimension_semantics=("arbitrary", "arbitrary"), vmem_limit_bytes=VMEM_LIMIT),
        name=f"prompt_mixer_l{layer}",
    )(sinks, hp, norm_mix, w_in, b_in, pool_w, pool_scale, conv_w, conv_b, conv_ng, conv_nb,
      w_conv_out, bias_tab, w_attn_out, w_out)


def _mlp_rows(x, g_ref, wup_ref, wdn_ref):
    u = _rms(x, g_ref[...]).astype(BF16)
    acc = x
    for c0 in range(0, D_FF, FF_CHUNK):
        up = jnp.maximum(_dot(u, wup_ref[:, c0:c0 + FF_CHUNK]), 0.0)
        acc = acc + _dot((up * up).astype(BF16), wdn_ref[c0:c0 + FF_CHUNK, :])
    return acc


def _mlp_body(h_ref, g_ref, wup_ref, wdn_ref, gf_ref, o_ref, *, final_norm):
    y = _mlp_rows(h_ref[...], g_ref, wup_ref, wdn_ref)
    if final_norm:
        y = _rms(y, gf_ref[...])
    o_ref[...] = y


def _mlp(layer, h2d, tm, norm_mlp, w_up, w_down, norm_final, final_norm):
    rows = h2d.shape[0]
    lsel3 = (layer, 0, 0)
    return pl.pallas_call(
        functools.partial(_mlp_body, final_norm=final_norm),
        grid=(rows // tm,),
        in_specs=[
            pl.BlockSpec((tm, D_MODEL), lambda i: (i, 0)),
            _const_spec((None, 1, D_MODEL), lsel3),
            _const_spec((None, D_MODEL, D_FF), lsel3),
            _const_spec((None, D_FF, D_MODEL), lsel3),
            _const_spec((1, D_MODEL), (0, 0)),
        ],
        out_specs=pl.BlockSpec((tm, D_MODEL), lambda i: (i, 0)),
        out_shape=jax.ShapeDtypeStruct((rows, D_MODEL), F32),
        compiler_params=pltpu.CompilerParams(
            dimension_semantics=("arbitrary",), vmem_limit_bytes=VMEM_LIMIT),
        name=f"mlp_l{layer}",
    )(h2d, norm_mlp, w_up, w_down, norm_final)


def _sample_in_body(h_ref, nm_ref, win_ref, bin_ref, pw_ref, ps_ref, cw_ref, cb_ref, cng_ref,
                    cnb_ref, wco_ref, sp_ref, sc_ref,
                    mrg_ref, g2_ref, q_ref, kn_ref, vn_ref, po_ref, co_ref,
                    exta, extc, *, sb, tdec):
    rows = sb * tdec
    x = h_ref[...]
    u = _rms(x, nm_ref[...]).astype(BF16)

    def proj(c0, width):
        return _dot(u, win_ref[:, c0:c0 + width]) + bin_ref[:, c0:c0 + width]

    def gate(i):
        return jax.nn.sigmoid(proj(C_GATE + i * D_MODEL, D_MODEL))

    a = proj(C_POOL, POOL_WIDTH)
    exta[:, 1:POOL_CARRY, :] = sp_ref[...]
    exta[:, POOL_CARRY:POOL_CARRY + tdec, :] = a.reshape(sb, tdec, POOL_WIDTH)
    ya_parts = []
    for gi, win in enumerate(POOL_WINDOWS):
        c0 = gi * POOL_GW
        own = a[:, c0:c0 + POOL_GW]
        s = own
        for j in range(1, win):
            s = s + exta[:, POOL_CARRY - j:POOL_CARRY - j + tdec, c0:c0 + POOL_GW].reshape(
                rows, POOL_GW)
        r = (s / float(win) - own).astype(BF16)
        ya_parts.append(_dot(r, pw_ref[gi]))
    ya = jnp.concatenate(ya_parts, axis=1) * ps_ref[...]
    po_ref[...] = exta[:, POOL_CARRY + tdec - POOL_HIST:POOL_CARRY + tdec, :]
    merged = gate(0) * ya

    c = proj(C_GLU, 2 * CONV_DIM)
    glu = c[:, :CONV_DIM] * jax.nn.sigmoid(c[:, CONV_DIM:])
    first = CONV_CARRY - CONV_HIST
    extc[:, first:CONV_CARRY, :] = sc_ref[...]
    extc[:, CONV_CARRY:CONV_CARRY + tdec, :] = glu.reshape(sb, tdec, CONV_DIM)
    acc = jnp.zeros((rows, CONV_DIM), F32)
    for j in range(CONV_WIDTH):
        acc = acc + extc[:, first + j:first + j + tdec, :].reshape(rows, CONV_DIM) * cw_ref[j:j + 1, :]
    y = acc + cb_ref[...]
    s_act = _layernorm_silu(y, cng_ref[...], cnb_ref[...]).astype(BF16)
    co_ref[...] = extc[:, CONV_CARRY + tdec - CONV_HIST:CONV_CARRY + tdec, :]
    merged = merged + gate(1) * _dot(s_act, wco_ref[...])

    mrg_ref[...] = merged
    g2_ref[...] = gate(2)
    q_ref[...] = proj(C_Q, N_HEADS * HEAD_DIM) * (HEAD_DIM ** -0.5)
    kv = proj(C_K, 2 * KV_DIM)
    kn_ref[...] = kv[:, :KV_DIM]
    vn_ref[...] = kv[:, KV_DIM:]


def _sample_in(layer, hs, sb, tdec, norm_mix, w_in, b_in, pool_w, pool_scale, conv_w, conv_b,
               conv_ng, conv_nb, w_conv_out, state_pool, state_conv):
    rows_all = hs.shape[0]
    n_seq = rows_all // tdec
    rows = sb * tdec
    lsel3 = (layer, 0, 0)
    row_spec = lambda w: pl.BlockSpec((rows, w), lambda i: (i, 0))
    in_specs = [
        row_spec(D_MODEL),
        _const_spec((None, 1, D_MODEL), lsel3),
        _const_spec((None, D_MODEL, D_IN), lsel3),
        _const_spec((None, 1, D_IN), lsel3),
        _const_spec((None, len(POOL_WINDOWS), POOL_GW, POOL_OUT_GW), (layer, 0, 0, 0)),
        _const_spec((None, 1, D_MODEL), lsel3),
        _const_spec((None, CONV_WIDTH, CONV_DIM), lsel3),
        _const_spec((None, 1, CONV_DIM), lsel3),
        _const_spec((None, 1, CONV_DIM), lsel3),
        _const_spec((None, 1, CONV_DIM), lsel3),
        _const_spec((None, CONV_DIM, D_MODEL), lsel3),
        pl.BlockSpec((None, sb, POOL_HIST, POOL_WIDTH), lambda i: (layer, i, 0, 0)),
        pl.BlockSpec((None, sb, CONV_HIST, CONV_DIM), lambda i: (layer, i, 0, 0)),
    ]
    out_shape = [
        jax.ShapeDtypeStruct((rows_all, D_MODEL), F32),
        jax.ShapeDtypeStruct((rows_all, D_MODEL), F32),
        jax.ShapeDtypeStruct((rows_all, D_MODEL), F32),
        jax.ShapeDtypeStruct((rows_all, KV_DIM), F32),
        jax.ShapeDtypeStruct((rows_all, KV_DIM), F32),
        jax.ShapeDtypeStruct((n_seq, POOL_HIST, POOL_WIDTH), F32),
        jax.ShapeDtypeStruct((n_seq, CONV_HIST, CONV_DIM), F32),
    ]
    out_specs = [
        row_spec(D_MODEL), row_spec(D_MODEL), row_spec(D_MODEL), row_spec(KV_DIM), row_spec(KV_DIM),
        pl.BlockSpec((sb, POOL_HIST, POOL_WIDTH), lambda i: (i, 0, 0)),
        pl.BlockSpec((sb, CONV_HIST, CONV_DIM), lambda i: (i, 0, 0)),
    ]
    scratch = [
        pltpu.VMEM((sb, POOL_CARRY + tdec, POOL_WIDTH), F32),
        pltpu.VMEM((sb, CONV_CARRY + tdec, CONV_DIM), F32),
    ]
    return pl.pallas_call(
        functools.partial(_sample_in_body, sb=sb, tdec=tdec),
        grid=(n_seq // sb,),
        in_specs=in_specs,
        out_specs=out_specs,
        out_shape=out_shape,
        scratch_shapes=scratch,
        compiler_params=pltpu.CompilerParams(
            dimension_semantics=("arbitrary",), vmem_limit_bytes=VMEM_LIMIT),
        name=f"sample_in_l{layer}",
    )(hs, norm_mix, w_in, b_in, pool_w, pool_scale, conv_w, conv_b, conv_ng, conv_nb,
      w_conv_out, state_pool, state_conv)


def _sample_attn_body(q_ref, kn_ref, vn_ref, ck_ref, cv_ref, bias_ref, sink_ref,
                      o_ref, ko_ref, vo_ref, qbd, kx, vx, *, sb, tdec):
    n_keys = WINDOW + tdec

    @pl.when(pl.program_id(0) == 0)
    def _():
        qbd[...] = jnp.zeros(qbd.shape, F32)
        kx[...] = jnp.zeros(kx.shape, F32)
        vx[...] = jnp.zeros(vx.shape, F32)

    def one_sequence(s, carry):
        r0 = pl.multiple_of(s * tdec, tdec)
        qs = q_ref[pl.ds(r0, tdec), :]
        kn = kn_ref[pl.ds(r0, tdec), :]
        vn = vn_ref[pl.ds(r0, tdec), :]
        kc = ck_ref[s]
        vc = cv_ref[s]
        for h in range(N_HEADS):
            g_kv = h // GQA_GROUP
            qbd[h * tdec:(h + 1) * tdec, g_kv * HEAD_DIM:(g_kv + 1) * HEAD_DIM] = (
                qs[:, h * HEAD_DIM:(h + 1) * HEAD_DIM])
        kx[0:WINDOW, :] = kc
        kx[WINDOW:n_keys, :] = kn
        vx[0:WINDOW, :] = vc
        vx[WINDOW:n_keys, :] = vn
        logits = _dot_nt(qbd[...].astype(BF16), kx[...].astype(BF16)) + bias_ref[...]
        p = _softmax_with_sink(logits, sink_ref[...]).astype(BF16)
        o_full = _dot(p, vx[...].astype(BF16))
        for h in range(N_HEADS):
            g_kv = h // GQA_GROUP
            o_ref[pl.ds(r0, tdec), h * HEAD_DIM:(h + 1) * HEAD_DIM] = (
                o_full[h * tdec:(h + 1) * tdec, g_kv * HEAD_DIM:(g_kv + 1) * HEAD_DIM])
        ko_ref[s, 0:WINDOW - tdec, :] = kc[tdec:, :]
        ko_ref[s, WINDOW - tdec:WINDOW, :] = kn
        vo_ref[s, 0:WINDOW - tdec, :] = vc[tdec:, :]
        vo_ref[s, WINDOW - tdec:WINDOW, :] = vn
        return carry

    lax.fori_loop(0, sb, one_sequence, 0)


def _sample_attn(layer, q, kn, vn, cache_k, cache_v, bias_s, sink_col, sb, tdec):
    rows_all = q.shape[0]
    n_seq = rows_all // tdec
    rows = sb * tdec
    key_pad = 2 * WINDOW
    row_spec = lambda w: pl.BlockSpec((rows, w), lambda i: (i, 0))
    cache_spec = pl.BlockSpec((None, sb, WINDOW, KV_DIM), lambda i: (layer, i, 0, 0))
    state_spec = pl.BlockSpec((sb, WINDOW, KV_DIM), lambda i: (i, 0, 0))
    return pl.pallas_call(
        functools.partial(_sample_attn_body, sb=sb, tdec=tdec),
        grid=(n_seq // sb,),
        in_specs=[
            row_spec(D_MODEL), row_spec(KV_DIM), row_spec(KV_DIM), cache_spec, cache_spec,
            _const_spec((N_HEADS * tdec, key_pad), (0, 0)),
            _const_spec((None, N_HEADS * tdec, 1), (layer, 0, 0)),
        ],
        out_specs=[row_spec(D_MODEL), state_spec, state_spec],
        out_shape=[
            jax.ShapeDtypeStruct((rows_all, D_MODEL), F32),
            jax.ShapeDtypeStruct((n_seq, WINDOW, KV_DIM), F32),
            jax.ShapeDtypeStruct((n_seq, WINDOW, KV_DIM), F32),
        ],
        scratch_shapes=[
            pltpu.VMEM((N_HEADS * tdec, KV_DIM), F32),
            pltpu.VMEM((key_pad, KV_DIM), F32),
            pltpu.VMEM((key_pad, KV_DIM), F32),
        ],
        compiler_params=pltpu.CompilerParams(
            dimension_semantics=("arbitrary",), vmem_limit_bytes=VMEM_LIMIT),
        name=f"sample_attn_l{layer}",
    )(q, kn, vn, cache_k, cache_v, bias_s, sink_col)


def _sample_out_body(h_ref, mrg_ref, g2_ref, o_ref, wao_ref, wo_ref, g_ref, wup_ref, wdn_ref,
                     gf_ref, out_ref, *, final_norm):
    yc = _dot(o_ref[...].astype(BF16), wao_ref[...])
    merged = mrg_ref[...] + g2_ref[...] * yc
    h1 = h_ref[...] + _dot(merged.astype(BF16), wo_ref[...])
    y = _mlp_rows(h1, g_ref, wup_ref, wdn_ref)
    if final_norm:
        y = _rms(y, gf_ref[...])
    out_ref[...] = y


def _sample_out(layer, hs, mrg, g2, o, tm, w_attn_out, w_out, norm_mlp, w_up, w_down,
                norm_final, final_norm):
    rows = hs.shape[0]
    lsel3 = (layer, 0, 0)
    row_spec = pl.BlockSpec((tm, D_MODEL), lambda i: (i, 0))
    return pl.pallas_call(
        functools.partial(_sample_out_body, final_norm=final_norm),
        grid=(rows // tm,),
        in_specs=[
            row_spec, row_spec, row_spec, row_spec,
            _const_spec((None, D_MODEL, D_MODEL), lsel3),
            _const_spec((None, D_MODEL, D_MODEL), lsel3),
            _const_spec((None, 1, D_MODEL), lsel3),
            _const_spec((None, D_MODEL, D_FF), lsel3),
            _const_spec((None, D_FF, D_MODEL), lsel3),
            _const_spec((1, D_MODEL), (0, 0)),
        ],
        out_specs=row_spec,
        out_shape=jax.ShapeDtypeStruct((rows, D_MODEL), F32),
        compiler_params=pltpu.CompilerParams(
            dimension_semantics=("arbitrary",), vmem_limit_bytes=VMEM_LIMIT),
        name=f"sample_out_l{layer}",
    )(hs, mrg, g2, o, w_attn_out, w_out, norm_mlp, w_up, w_down, norm_final)


def _forward(x_prompt, x_sample, state_pool, state_conv, cache_k, cache_v, meta_tokens,
             rel_bias, norm_mix, w_in, b_in, pool_w, pool_scale, conv_w, conv_b, conv_norm_g,
             conv_norm_b, w_conv_out, attn_sinks, w_attn_out, w_out, norm_mlp, w_up, w_down,
             norm_final, *, past_len, tm_prompt, tm_sample, sb_in, sb_attn):
    depth = w_in.shape[0]
    nb, seq, _ = x_prompt.shape
    n_seq, tdec, _ = x_sample.shape
    assert past_len >= POOL_HIST and past_len >= WINDOW
    t_valid = seq + N_META
    tp = -(-t_valid // tm_prompt) * tm_prompt

    meta = jnp.broadcast_to(meta_tokens[None].astype(F32), (nb, N_META, D_MODEL))
    hp = jnp.concatenate(
        [meta, x_prompt, jnp.zeros((nb, tp - t_valid, D_MODEL), F32)], axis=1)
    hs = x_sample.reshape(n_seq * tdec, D_MODEL)

    row3 = lambda a: a.reshape(depth, 1, a.shape[-1])
    w_in_b, pool_w_b, wco_b = w_in.astype(BF16), pool_w.astype(BF16), w_conv_out.astype(BF16)
    wao_b, wo_b = w_attn_out.astype(BF16), w_out.astype(BF16)
    w_up_b, w_down_b = w_up.astype(BF16), w_down.astype(BF16)
    nm3, bin3, ps3 = row3(norm_mix), row3(b_in), row3(pool_scale)
    cb3, cng3, cnb3, nmlp3 = row3(conv_b), row3(conv_norm_g), row3(conv_norm_b), row3(norm_mlp)
    nf2 = norm_final.reshape(1, D_MODEL)

    dist_p = WINDOW + np.arange(WINDOW)[:, None] - np.arange(2 * WINDOW)[None, :]
    has_prev = np.ones((1, 2 * WINDOW), bool)
    no_prev = (np.arange(2 * WINDOW) >= WINDOW)[None, :]
    bias_tab = jnp.stack([_banded_bias(rel_bias, dist_p, has_prev),
                          _banded_bias(rel_bias, dist_p, no_prev)])
    bias_tab = bias_tab.reshape(2, N_KV_HEADS, GQA_GROUP * WINDOW, 2 * WINDOW)

    dist_s = WINDOW + np.arange(tdec)[:, None] - np.arange(2 * WINDOW)[None, :]
    real_key = (np.arange(2 * WINDOW) < WINDOW + tdec)[None, :]
    bias_s = _banded_bias(rel_bias, dist_s, real_key).reshape(N_HEADS * tdec, 2 * WINDOW)
    sink_col = jnp.repeat(attn_sinks.astype(F32), tdec, axis=1).reshape(depth, N_HEADS * tdec, 1)

    cache_k2 = cache_k.reshape(depth, n_seq, WINDOW, KV_DIM)
    cache_v2 = cache_v.reshape(depth, n_seq, WINDOW, KV_DIM)
    sinks = attn_sinks.astype(F32)

    pool_p, pool_s, conv_p, conv_s, k_p, k_s, v_p, v_s = [], [], [], [], [], [], [], []
    for l in range(depth):
        last = l == depth - 1
        hp, pst, cst, kst, vst = _prompt_mixer(
            l, hp, t_valid, tm_prompt, sinks, nm3, w_in_b, bin3, pool_w_b, ps3, conv_w, cb3,
            cng3, cnb3, wco_b, bias_tab, wao_b, wo_b)
        hp = _mlp(l, hp.reshape(nb * tp, D_MODEL), tm_prompt, nmlp3, w_up_b, w_down_b, nf2,
                  last).reshape(nb, tp, D_MODEL)
        pool_p.append(pst[:, POOL_CARRY - POOL_HIST:])
        conv_p.append(cst[:, CONV_CARRY - CONV_HIST:])
        k_p.append(kst.reshape(nb, WINDOW, N_KV_HEADS, HEAD_DIM))
        v_p.append(vst.reshape(nb, WINDOW, N_KV_HEADS, HEAD_DIM))

        mrg, g2, q, kn, vn, po, co = _sample_in(
            l, hs, sb_in, tdec, nm3, w_in_b, bin3, pool_w_b, ps3, conv_w, cb3, cng3, cnb3,
            wco_b, state_pool, state_conv)
        o, ko, vo = _sample_attn(l, q, kn, vn, cache_k2, cache_v2, bias_s, sink_col, sb_attn,
                                 tdec)
        hs = _sample_out(l, hs, mrg, g2, o, tm_sample, wao_b, wo_b, nmlp3, w_up_b, w_down_b,
                         nf2, last)
        pool_s.append(po)
        conv_s.append(co)
        k_s.append(ko.reshape(n_seq, WINDOW, N_KV_HEADS, HEAD_DIM))
        v_s.append(vo.reshape(n_seq, WINDOW, N_KV_HEADS, HEAD_DIM))

    y_prompt = hp[:, N_META:t_valid]
    y_sample = hs.reshape(n_seq, tdec, D_MODEL)
    return (y_prompt, y_sample, jnp.stack(pool_p), jnp.stack(pool_s), jnp.stack(conv_p),
            jnp.stack(conv_s), jnp.stack(k_p), jnp.stack(k_s), jnp.stack(v_p), jnp.stack(v_s))


def kernel(x_prompt, x_sample, state_pool, state_conv, cache_k, cache_v, meta_tokens, rel_bias, norm_mix, w_in, b_in, pool_w, pool_scale, conv_w, conv_b, conv_norm_g, conv_norm_b, w_conv_out, attn_sinks, w_attn_out, w_out, norm_mlp, w_up, w_down, norm_final):
    return _forward(
        x_prompt, x_sample, state_pool, state_conv, cache_k, cache_v, meta_tokens, rel_bias,
        norm_mix, w_in, b_in, pool_w, pool_scale, conv_w, conv_b, conv_norm_g, conv_norm_b,
        w_conv_out, attn_sinks, w_attn_out, w_out, norm_mlp, w_up, w_down, norm_final,
        past_len=8192, tm_prompt=640, tm_sample=512, sb_in=32, sb_attn=16)
```

```python
import functools

import numpy as np
import jax
import jax.numpy as jnp
from jax import lax
from jax.experimental import pallas as pl
from jax.experimental.pallas import tpu as pltpu

F32 = jnp.float32
BF16 = jnp.bfloat16

D_MODEL = 1024
N_META = 16
POOL_WINDOWS = (2, 4, 8, 16)
POOL_WIDTH = 512
POOL_GW = 128
POOL_OUT_GW = 256
POOL_HIST = 15
CONV_DIM = 512
CONV_WIDTH = 31
CONV_HIST = 30
HEAD_DIM = 64
N_HEADS = 16
N_KV_HEADS = 4
GQA_GROUP = 4
KV_DIM = N_KV_HEADS * HEAD_DIM
WINDOW = 128
N_BUCKETS = 32
MAX_DISTANCE = 128
D_FF = 4096
NORM_EPS = 1e-6
C_POOL = 0
C_GLU = C_POOL + POOL_WIDTH
C_Q = C_GLU + 2 * CONV_DIM
C_K = C_Q + N_HEADS * HEAD_DIM
C_V = C_K + KV_DIM
C_GATE = C_V + KV_DIM
D_IN = C_GATE + 3 * D_MODEL

SUBLANES = 8
LANES = 128
POOL_CARRY = 16
CONV_CARRY = 32
FF_CHUNK = 1024
CONV_ROWS = 128
VMEM_LIMIT = 60 * 1024 * 1024


def _dot(a, b):
    return jnp.dot(a, b, preferred_element_type=F32)


def _dot_nt(a, b):
    return lax.dot_general(a, b, (((1,), (1,)), ((), ())), preferred_element_type=F32)


def _rms(x, g):
    ms = jnp.mean(x * x, axis=-1, keepdims=True)
    return x * lax.rsqrt(ms + NORM_EPS) * g


def _layernorm_silu(y, g, b):
    mu = jnp.mean(y, axis=-1, keepdims=True)
    yc = y - mu
    var = jnp.mean(yc * yc, axis=-1, keepdims=True)
    yn = yc * lax.rsqrt(var + NORM_EPS) * g + b
    return yn * jax.nn.sigmoid(yn)


def _softmax_with_sink(logits, sink):
    m = jnp.maximum(jnp.max(logits, axis=-1, keepdims=True), sink)
    p = jnp.exp(logits - m)
    denom = jnp.sum(p, axis=-1, keepdims=True) + jnp.exp(sink - m)
    return p * (1.0 / denom)


def _const_spec(block_shape, index):
    return pl.BlockSpec(block_shape, lambda *_: index, pipeline_mode=pl.Buffered(1))


def _t5_bucket(dist):
    n = np.maximum(dist, 0)
    exact = N_BUCKETS // 2
    large = exact + (np.log(np.maximum(n, 1) / exact) / np.log(MAX_DISTANCE / exact)
                     * (N_BUCKETS - exact)).astype(np.int32)
    large = np.minimum(large, N_BUCKETS - 1)
    return np.where(n < exact, n, large).astype(np.int32)


def _banded_bias(rel_bias):
    n_diag = 3 * WINDOW - 1
    per_dist = rel_bias[_t5_bucket(np.arange(WINDOW + 1))].astype(F32).T
    profile = jnp.full((N_HEADS, n_diag + 1), -jnp.inf, F32)
    profile = profile.at[:, WINDOW - 1:2 * WINDOW].set(per_dist[:, ::-1])
    skew = jnp.tile(profile, (1, WINDOW))[:, :WINDOW * n_diag].reshape(N_HEADS, WINDOW, n_diag)
    return skew[:, :, WINDOW - 1:WINDOW - 1 + 2 * WINDOW]


def _prompt_mixer_body(sinks_ref, h_ref, nm_ref, win_ref, bin_ref, pw_ref, ps_ref, cw_ref,
                       cb_ref, cng_ref, cnb_ref, wco_ref, bias_ref, wao_ref, wo_ref,
                       ho_ref, pst_ref, cst_ref, kst_ref, vst_ref,
                       u_scr, exta, extc, kbuf, vbuf, q_scr, o_scr, s_scr, mrg, xsh,
                       *, layer, tm, last_tile, e_loc):
    t = pl.program_id(1)

    @pl.when(t == 0)
    def _():
        exta[0:POOL_CARRY, :] = jnp.zeros((POOL_CARRY, POOL_WIDTH), F32)
        extc[0:CONV_CARRY, :] = jnp.zeros((CONV_CARRY, CONV_DIM), F32)
        kbuf[0:WINDOW, :] = jnp.zeros((WINDOW, KV_DIM), BF16)
        vbuf[:, 0:WINDOW] = jnp.zeros((KV_DIM, WINDOW), BF16)

    @pl.when(t > 0)
    def _():
        exta[0:POOL_CARRY, :] = exta[tm:tm + POOL_CARRY, :]
        extc[0:CONV_CARRY, :] = extc[tm:tm + CONV_CARRY, :]
        kbuf[0:WINDOW, :] = kbuf[tm:tm + WINDOW, :]
        vbuf[:, 0:WINDOW] = vbuf[:, tm:tm + WINDOW]

    x = h_ref[...]
    u_scr[...] = _rms(x, nm_ref[...]).astype(BF16)

    def proj(c0, width):
        return _dot(u_scr[...], win_ref[:, c0:c0 + width]) + bin_ref[:, c0:c0 + width]

    def gate(i):
        return jax.nn.sigmoid(proj(C_GATE + i * D_MODEL, D_MODEL))

    a = proj(C_POOL, POOL_WIDTH)
    exta[POOL_CARRY:POOL_CARRY + tm, :] = a
    pos = t * tm + lax.broadcasted_iota(jnp.int32, (tm, 1), 0)
    for gi, win in enumerate(POOL_WINDOWS):
        c0 = gi * POOL_GW
        own = a[:, c0:c0 + POOL_GW]
        s = own
        for j in range(1, win):
            s = s + exta[POOL_CARRY - j:POOL_CARRY - j + tm, c0:c0 + POOL_GW]
        cnt = jnp.minimum(pos + 1, win).astype(F32)
        r = (s / cnt - own).astype(BF16)
        o0 = gi * POOL_OUT_GW
        ya = _dot(r, pw_ref[gi]) * ps_ref[:, o0:o0 + POOL_OUT_GW]
        mrg[:, o0:o0 + POOL_OUT_GW] = ya
    mrg[...] = gate(0) * mrg[...]

    c = proj(C_GLU, 2 * CONV_DIM)
    extc[CONV_CARRY:CONV_CARRY + tm, :] = c[:, :CONV_DIM] * jax.nn.sigmoid(c[:, CONV_DIM:])
    q_scr[...] = (proj(C_Q, N_HEADS * HEAD_DIM) * (HEAD_DIM ** -0.5)).astype(BF16)
    kv = proj(C_K, 2 * KV_DIM)
    kbuf[WINDOW:WINDOW + tm, :] = kv[:, :KV_DIM].astype(BF16)
    vbuf[:, WINDOW:WINDOW + tm] = kv[:, KV_DIM:].T.astype(BF16)

    @pl.when(t == last_tile)
    def _():
        pst_ref[...] = exta[e_loc:e_loc + POOL_CARRY, :]
        cst_ref[...] = extc[e_loc:e_loc + CONV_CARRY, :]
        kst_ref[...] = kv[e_loc - WINDOW:e_loc, :KV_DIM]
        vst_ref[...] = kv[e_loc - WINDOW:e_loc, KV_DIM:]

    first = CONV_CARRY - CONV_HIST
    last_off = first + CONV_WIDTH - 1

    def conv_chunk(ci):
        r0 = ci * CONV_ROWS
        cols = []
        for li in range(CONV_DIM // LANES):
            c0 = li * LANES
            slot = ((ci * (CONV_DIM // LANES) + li) % 2) * SUBLANES
            acc = jnp.zeros((CONV_ROWS, LANES), F32)
            for sh in range(SUBLANES):
                taps = [a0 for a0 in range(0, last_off + 1, SUBLANES) if first <= a0 + sh <= last_off]
                n = taps[-1] + CONV_ROWS
                if sh:
                    xsh[slot + sh, 0:n, :] = extc[r0 + sh:r0 + sh + n, c0:c0 + LANES]
                for a0 in taps:
                    w_row = cw_ref[a0 + sh - first:a0 + sh - first + 1, c0:c0 + LANES]
                    if sh:
                        acc = acc + xsh[slot + sh, a0:a0 + CONV_ROWS, :] * w_row
                    else:
                        acc = acc + extc[r0 + a0:r0 + a0 + CONV_ROWS, c0:c0 + LANES] * w_row
            cols.append(acc)
        y = jnp.concatenate(cols, axis=1) + cb_ref[...]
        s_scr[r0:r0 + CONV_ROWS, :] = _layernorm_silu(y, cng_ref[...], cnb_ref[...]).astype(BF16)

    def attn_block(qb):
        r0 = qb * WINDOW
        sel = jnp.where(t == 0, 1, 0) if qb == 0 else 0
        outs = []
        for g_kv in range(N_KV_HEADS):
            heads = range(g_kv * GQA_GROUP, (g_kv + 1) * GQA_GROUP)
            q4 = jnp.concatenate(
                [q_scr[r0:r0 + WINDOW, h * HEAD_DIM:(h + 1) * HEAD_DIM] for h in heads], axis=0)
            kw = kbuf[r0:r0 + 2 * WINDOW, g_kv * HEAD_DIM:(g_kv + 1) * HEAD_DIM]
            logits = _dot_nt(kw, q4) + bias_ref[sel, g_kv]
            sink = jnp.concatenate(
                [jnp.full((1, WINDOW), sinks_ref[layer, h], F32) for h in heads], axis=1)
            m = jnp.maximum(jnp.max(logits, axis=0, keepdims=True), sink)
            p = jnp.exp(logits - m)
            denom = jnp.sum(p, axis=0, keepdims=True) + jnp.exp(sink - m)
            vw_t = vbuf[g_kv * HEAD_DIM:(g_kv + 1) * HEAD_DIM, r0:r0 + 2 * WINDOW]
            o_t = _dot(vw_t, p.astype(BF16)) * (1.0 / denom)
            outs.extend(o_t[:, g * WINDOW:(g + 1) * WINDOW] for g in range(GQA_GROUP))
        o_scr[r0:r0 + WINDOW, :] = jnp.concatenate(outs, axis=0).T.astype(BF16)

    for i in range(tm // CONV_ROWS):
        conv_chunk(i)
    for i in range(tm // WINDOW):
        attn_block(i)

    yb = _dot(s_scr[...], wco_ref[...])
    mrg[...] = mrg[...] + gate(1) * yb
    yc = _dot(o_scr[...], wao_ref[...])
    merged = mrg[...] + gate(2) * yc
    ho_ref[...] = x + _dot(merged.astype(BF16), wo_ref[...])


def _prompt_mixer(layer, hp, t_valid, tm, sinks, norm_mix, w_in, b_in, pool_w, pool_scale,
                  conv_w, conv_b, conv_ng, conv_nb, w_conv_out, bias_tab, w_attn_out, w_out):
    nb, tp, _ = hp.shape
    n_tiles = tp // tm
    last_tile = (t_valid - 1) // tm
    e_loc = t_valid - last_tile * tm
    assert last_tile == n_tiles - 1 and e_loc >= WINDOW and e_loc % SUBLANES == 0
    body = functools.partial(_prompt_mixer_body, layer=layer, tm=tm, last_tile=last_tile,
                             e_loc=e_loc)
    lsel3 = (layer, 0, 0)
    in_specs = [
        pl.BlockSpec(memory_space=pltpu.SMEM),
        pl.BlockSpec((None, tm, D_MODEL), lambda b, t: (b, t, 0)),
        _const_spec((None, 1, D_MODEL), lsel3),
        _const_spec((None, D_MODEL, D_IN), lsel3),
        _const_spec((None, 1, D_IN), lsel3),
        _const_spec((None, len(POOL_WINDOWS), POOL_GW, POOL_OUT_GW), (layer, 0, 0, 0)),
        _const_spec((None, 1, D_MODEL), lsel3),
        _const_spec((None, CONV_WIDTH, CONV_DIM), lsel3),
        _const_spec((None, 1, CONV_DIM), lsel3),
        _const_spec((None, 1, CONV_DIM), lsel3),
        _const_spec((None, 1, CONV_DIM), lsel3),
        _const_spec((None, CONV_DIM, D_MODEL), lsel3),
        _const_spec((2, N_KV_HEADS, 2 * WINDOW, GQA_GROUP * WINDOW), (0, 0, 0, 0)),
        _const_spec((None, D_MODEL, D_MODEL), lsel3),
        _const_spec((None, D_MODEL, D_MODEL), lsel3),
    ]
    out_shape = [
        jax.ShapeDtypeStruct((nb, tp, D_MODEL), F32),
        jax.ShapeDtypeStruct((nb, POOL_CARRY, POOL_WIDTH), F32),
        jax.ShapeDtypeStruct((nb, CONV_CARRY, CONV_DIM), F32),
        jax.ShapeDtypeStruct((nb, WINDOW, KV_DIM), F32),
        jax.ShapeDtypeStruct((nb, WINDOW, KV_DIM), F32),
    ]
    out_specs = [
        pl.BlockSpec((None, tm, D_MODEL), lambda b, t: (b, t, 0)),
        pl.BlockSpec((None, POOL_CARRY, POOL_WIDTH), lambda b, t: (b, 0, 0)),
        pl.BlockSpec((None, CONV_CARRY, CONV_DIM), lambda b, t: (b, 0, 0)),
        pl.BlockSpec((None, WINDOW, KV_DIM), lambda b, t: (b, 0, 0)),
        pl.BlockSpec((None, WINDOW, KV_DIM), lambda b, t: (b, 0, 0)),
    ]
    scratch = [
        pltpu.VMEM((tm, D_MODEL), BF16),
        pltpu.VMEM((POOL_CARRY + tm, POOL_WIDTH), F32),
        pltpu.VMEM((CONV_CARRY + tm, CONV_DIM), F32),
        pltpu.VMEM((WINDOW + tm, KV_DIM), BF16),
        pltpu.VMEM((KV_DIM, WINDOW + tm), BF16),
        pltpu.VMEM((tm, D_MODEL), BF16),
        pltpu.VMEM((tm, D_MODEL), BF16),
        pltpu.VMEM((tm, CONV_DIM), BF16),
        pltpu.VMEM((tm, D_MODEL), F32),
        pltpu.VMEM((2 * SUBLANES, CONV_ROWS + CONV_CARRY, LANES), F32),
    ]
    return pl.pallas_call(
        body,
        grid=(nb, n_tiles),
        in_specs=in_specs,
        out_specs=out_specs,
        out_shape=out_shape,
        scratch_shapes=scratch,
        compiler_params=pltpu.CompilerParams(
            dimension_semantics=("arbitrary", "arbitrary"), vmem_limit_bytes=VMEM_LIMIT),
        name=f"prompt_mixer_l{layer}",
    )(sinks, hp, norm_mix, w_in, b_in, pool_w, pool_scale, conv_w, conv_b, conv_ng, conv_nb,
      w_conv_out, bias_tab, w_attn_out, w_out)


def _mlp_rows(x, g_ref, wup_ref, wdn_ref):
    u = _rms(x, g_ref[...]).astype(BF16)
    acc = x
    for c0 in range(0, D_FF, FF_CHUNK):
        up = jnp.maximum(_dot(u, wup_ref[:, c0:c0 + FF_CHUNK]), 0.0)
        acc = acc + _dot((up * up).astype(BF16), wdn_ref[c0:c0 + FF_CHUNK, :])
    return acc


def _mlp_body(h_ref, g_ref, wup_ref, wdn_ref, gf_ref, o_ref, *, final_norm):
    y = _mlp_rows(h_ref[...], g_ref, wup_ref, wdn_ref)
    if final_norm:
        y = _rms(y, gf_ref[...])
    o_ref[...] = y


def _mlp(layer, h2d, tm, norm_mlp, w_up, w_down, norm_final, final_norm):
    rows = h2d.shape[0]
    lsel3 = (layer, 0, 0)
    return pl.pallas_call(
        functools.partial(_mlp_body, final_norm=final_norm),
        grid=(rows // tm,),
        in_specs=[
            pl.BlockSpec((tm, D_MODEL), lambda i: (i, 0)),
            _const_spec((None, 1, D_MODEL), lsel3),
            _const_spec((None, D_MODEL, D_FF), lsel3),
            _const_spec((None, D_FF, D_MODEL), lsel3),
            _const_spec((1, D_MODEL), (0, 0)),
        ],
        out_specs=pl.BlockSpec((tm, D_MODEL), lambda i: (i, 0)),
        out_shape=jax.ShapeDtypeStruct((rows, D_MODEL), F32),
        compiler_params=pltpu.CompilerParams(
            dimension_semantics=("arbitrary",), vmem_limit_bytes=VMEM_LIMIT),
        name=f"mlp_l{layer}",
    )(h2d, norm_mlp, w_up, w_down, norm_final)


def _sample_in_body(h_ref, nm_ref, win_ref, bin_ref, pw_ref, ps_ref, cw_ref, cb_ref, cng_ref,
                    cnb_ref, wco_ref, sp_ref, sc_ref,
                    mrg_ref, g2_ref, q_ref, kn_ref, vn_ref, po_ref, co_ref,
                    exta, extc, *, sb, tdec):
    rows = sb * tdec
    x = h_ref[...]
    u = _rms(x, nm_ref[...]).astype(BF16)

    def proj(c0, width):
        return _dot(u, win_ref[:, c0:c0 + width]) + bin_ref[:, c0:c0 + width]

    def gate(i):
        return jax.nn.sigmoid(proj(C_GATE + i * D_MODEL, D_MODEL))

    a = proj(C_POOL, POOL_WIDTH)
    exta[:, 1:POOL_CARRY, :] = sp_ref[...]
    exta[:, POOL_CARRY:POOL_CARRY + tdec, :] = a.reshape(sb, tdec, POOL_WIDTH)
    ya_parts = []
    for gi, win in enumerate(POOL_WINDOWS):
        c0 = gi * POOL_GW
        own = a[:, c0:c0 + POOL_GW]
        s = own
        for j in range(1, win):
            s = s + exta[:, POOL_CARRY - j:POOL_CARRY - j + tdec, c0:c0 + POOL_GW].reshape(
                rows, POOL_GW)
        r = (s / float(win) - own).astype(BF16)
        ya_parts.append(_dot(r, pw_ref[gi]))
    ya = jnp.concatenate(ya_parts, axis=1) * ps_ref[...]
    po_ref[...] = exta[:, POOL_CARRY + tdec - POOL_HIST:POOL_CARRY + tdec, :]
    merged = gate(0) * ya

    c = proj(C_GLU, 2 * CONV_DIM)
    glu = c[:, :CONV_DIM] * jax.nn.sigmoid(c[:, CONV_DIM:])
    first = CONV_CARRY - CONV_HIST
    extc[:, first:CONV_CARRY, :] = sc_ref[...]
    extc[:, CONV_CARRY:CONV_CARRY + tdec, :] = glu.reshape(sb, tdec, CONV_DIM)
    acc = jnp.zeros((rows, CONV_DIM), F32)
    for j in range(CONV_WIDTH):
        acc = acc + extc[:, first + j:first + j + tdec, :].reshape(rows, CONV_DIM) * cw_ref[j:j + 1, :]
    y = acc + cb_ref[...]
    s_act = _layernorm_silu(y, cng_ref[...], cnb_ref[...]).astype(BF16)
    co_ref[...] = extc[:, CONV_CARRY + tdec - CONV_HIST:CONV_CARRY + tdec, :]
    merged = merged + gate(1) * _dot(s_act, wco_ref[...])

    mrg_ref[...] = merged
    g2_ref[...] = gate(2)
    q_ref[...] = proj(C_Q, N_HEADS * HEAD_DIM) * (HEAD_DIM ** -0.5)
    kv = proj(C_K, 2 * KV_DIM)
    kn_ref[...] = kv[:, :KV_DIM]
    vn_ref[...] = kv[:, KV_DIM:]


def _sample_in(layer, hs, sb, tdec, norm_mix, w_in, b_in, pool_w, pool_scale, conv_w, conv_b,
               conv_ng, conv_nb, w_conv_out, state_pool, state_conv):
    rows_all = hs.shape[0]
    n_seq = rows_all // tdec
    rows = sb * tdec
    lsel3 = (layer, 0, 0)
    row_spec = lambda w: pl.BlockSpec((rows, w), lambda i: (i, 0))
    in_specs = [
        row_spec(D_MODEL),
        _const_spec((None, 1, D_MODEL), lsel3),
        _const_spec((None, D_MODEL, D_IN), lsel3),
        _const_spec((None, 1, D_IN), lsel3),
        _const_spec((None, len(POOL_WINDOWS), POOL_GW, POOL_OUT_GW), (layer, 0, 0, 0)),
        _const_spec((None, 1, D_MODEL), lsel3),
        _const_spec((None, CONV_WIDTH, CONV_DIM), lsel3),
        _const_spec((None, 1, CONV_DIM), lsel3),
        _const_spec((None, 1, CONV_DIM), lsel3),
        _const_spec((None, 1, CONV_DIM), lsel3),
        _const_spec((None, CONV_DIM, D_MODEL), lsel3),
        pl.BlockSpec((None, sb, POOL_HIST, POOL_WIDTH), lambda i: (layer, i, 0, 0)),
        pl.BlockSpec((None, sb, CONV_HIST, CONV_DIM), lambda i: (layer, i, 0, 0)),
    ]
    out_shape = [
        jax.ShapeDtypeStruct((rows_all, D_MODEL), F32),
        jax.ShapeDtypeStruct((rows_all, D_MODEL), F32),
        jax.ShapeDtypeStruct((rows_all, D_MODEL), F32),
        jax.ShapeDtypeStruct((rows_all, KV_DIM), F32),
        jax.ShapeDtypeStruct((rows_all, KV_DIM), F32),
        jax.ShapeDtypeStruct((n_seq, POOL_HIST, POOL_WIDTH), F32),
        jax.ShapeDtypeStruct((n_seq, CONV_HIST, CONV_DIM), F32),
    ]
    out_specs = [
        row_spec(D_MODEL), row_spec(D_MODEL), row_spec(D_MODEL), row_spec(KV_DIM), row_spec(KV_DIM),
        pl.BlockSpec((sb, POOL_HIST, POOL_WIDTH), lambda i: (i, 0, 0)),
        pl.BlockSpec((sb, CONV_HIST, CONV_DIM), lambda i: (i, 0, 0)),
    ]
    scratch = [
        pltpu.VMEM((sb, POOL_CARRY + tdec, POOL_WIDTH), F32),
        pltpu.VMEM((sb, CONV_CARRY + tdec, CONV_DIM), F32),
    ]
    return pl.pallas_call(
        functools.partial(_sample_in_body, sb=sb, tdec=tdec),
        grid=(n_seq // sb,),
        in_specs=in_specs,
        out_specs=out_specs,
        out_shape=out_shape,
        scratch_shapes=scratch,
        compiler_params=pltpu.CompilerParams(
            dimension_semantics=("arbitrary",), vmem_limit_bytes=VMEM_LIMIT),
        name=f"sample_in_l{layer}",
    )(hs, norm_mix, w_in, b_in, pool_w, pool_scale, conv_w, conv_b, conv_ng, conv_nb,
      w_conv_out, state_pool, state_conv)


def _sample_attn_body(q_ref, kn_ref, vn_ref, ck_ref, cv_ref, bias_ref, sink_ref,
                      o_ref, ko_ref, vo_ref, qbd, kx, vx, *, sb, tdec):
    n_keys = WINDOW + tdec

    @pl.when(pl.program_id(0) == 0)
    def _():
        qbd[...] = jnp.zeros(qbd.shape, F32)
        kx[...] = jnp.zeros(kx.shape, F32)
        vx[...] = jnp.zeros(vx.shape, F32)

    def one_sequence(s, carry):
        r0 = pl.multiple_of(s * tdec, tdec)
        qs = q_ref[pl.ds(r0, tdec), :]
        kn = kn_ref[pl.ds(r0, tdec), :]
        vn = vn_ref[pl.ds(r0, tdec), :]
        kc = ck_ref[s]
        vc = cv_ref[s]
        for h in range(N_HEADS):
            g_kv = h // GQA_GROUP
            qbd[h * tdec:(h + 1) * tdec, g_kv * HEAD_DIM:(g_kv + 1) * HEAD_DIM] = (
                qs[:, h * HEAD_DIM:(h + 1) * HEAD_DIM])
        kx[0:WINDOW, :] = kc
        kx[WINDOW:n_keys, :] = kn
        vx[0:WINDOW, :] = vc
        vx[WINDOW:n_keys, :] = vn
        logits = _dot_nt(qbd[...].astype(BF16), kx[...].astype(BF16)) + bias_ref[...]
        p = _softmax_with_sink(logits, sink_ref[...]).astype(BF16)
        o_full = _dot(p, vx[...].astype(BF16))
        for h in range(N_HEADS):
            g_kv = h // GQA_GROUP
            o_ref[pl.ds(r0, tdec), h * HEAD_DIM:(h + 1) * HEAD_DIM] = (
                o_full[h * tdec:(h + 1) * tdec, g_kv * HEAD_DIM:(g_kv + 1) * HEAD_DIM])
        ko_ref[s, 0:WINDOW - tdec, :] = kc[tdec:, :]
        ko_ref[s, WINDOW - tdec:WINDOW, :] = kn
        vo_ref[s, 0:WINDOW - tdec, :] = vc[tdec:, :]
        vo_ref[s, WINDOW - tdec:WINDOW, :] = vn
        return carry

    lax.fori_loop(0, sb, one_sequence, 0)


def _sample_attn(layer, q, kn, vn, cache_k, cache_v, bias_s, sink_col, sb, tdec):
    rows_all = q.shape[0]
    n_seq = rows_all // tdec
    rows = sb * tdec
    key_pad = 2 * WINDOW
    row_spec = lambda w: pl.BlockSpec((rows, w), lambda i: (i, 0))
    cache_spec = pl.BlockSpec((None, sb, WINDOW, KV_DIM), lambda i: (layer, i, 0, 0))
    state_spec = pl.BlockSpec((sb, WINDOW, KV_DIM), lambda i: (i, 0, 0))
    return pl.pallas_call(
        functools.partial(_sample_attn_body, sb=sb, tdec=tdec),
        grid=(n_seq // sb,),
        in_specs=[
            row_spec(D_MODEL), row_spec(KV_DIM), row_spec(KV_DIM), cache_spec, cache_spec,
            _const_spec((N_HEADS * tdec, key_pad), (0, 0)),
            _const_spec((None, N_HEADS * tdec, 1), (layer, 0, 0)),
        ],
        out_specs=[row_spec(D_MODEL), state_spec, state_spec],
        out_shape=[
            jax.ShapeDtypeStruct((rows_all, D_MODEL), F32),
            jax.ShapeDtypeStruct((n_seq, WINDOW, KV_DIM), F32),
            jax.ShapeDtypeStruct((n_seq, WINDOW, KV_DIM), F32),
        ],
        scratch_shapes=[
            pltpu.VMEM((N_HEADS * tdec, KV_DIM), F32),
            pltpu.VMEM((key_pad, KV_DIM), F32),
            pltpu.VMEM((key_pad, KV_DIM), F32),
        ],
        compiler_params=pltpu.CompilerParams(
            dimension_semantics=("arbitrary",), vmem_limit_bytes=VMEM_LIMIT),
        name=f"sample_attn_l{layer}",
    )(q, kn, vn, cache_k, cache_v, bias_s, sink_col)


def _sample_out_body(h_ref, mrg_ref, g2_ref, o_ref, wao_ref, wo_ref, g_ref, wup_ref, wdn_ref,
                     gf_ref, out_ref, *, final_norm):
    yc = _dot(o_ref[...].astype(BF16), wao_ref[...])
    merged = mrg_ref[...] + g2_ref[...] * yc
    h1 = h_ref[...] + _dot(merged.astype(BF16), wo_ref[...])
    y = _mlp_rows(h1, g_ref, wup_ref, wdn_ref)
    if final_norm:
        y = _rms(y, gf_ref[...])
    out_ref[...] = y


def _sample_out(layer, hs, mrg, g2, o, tm, w_attn_out, w_out, norm_mlp, w_up, w_down,
                norm_final, final_norm):
    rows = hs.shape[0]
    lsel3 = (layer, 0, 0)
    row_spec = pl.BlockSpec((tm, D_MODEL), lambda i: (i, 0))
    return pl.pallas_call(
        functools.partial(_sample_out_body, final_norm=final_norm),
        grid=(rows // tm,),
        in_specs=[
            row_spec, row_spec, row_spec, row_spec,
            _const_spec((None, D_MODEL, D_MODEL), lsel3),
            _const_spec((None, D_MODEL, D_MODEL), lsel3),
            _const_spec((None, 1, D_MODEL), lsel3),
            _const_spec((None, D_MODEL, D_FF), lsel3),
            _const_spec((None, D_FF, D_MODEL), lsel3),
            _const_spec((1, D_MODEL), (0, 0)),
        ],
        out_specs=row_spec,
        out_shape=jax.ShapeDtypeStruct((rows, D_MODEL), F32),
        compiler_params=pltpu.CompilerParams(
            dimension_semantics=("arbitrary",), vmem_limit_bytes=VMEM_LIMIT),
        name=f"sample_out_l{layer}",
    )(hs, mrg, g2, o, w_attn_out, w_out, norm_mlp, w_up, w_down, norm_final)


def _forward(x_prompt, x_sample, state_pool, state_conv, cache_k, cache_v, meta_tokens,
             rel_bias, norm_mix, w_in, b_in, pool_w, pool_scale, conv_w, conv_b, conv_norm_g,
             conv_norm_b, w_conv_out, attn_sinks, w_attn_out, w_out, norm_mlp, w_up, w_down,
             norm_final, *, past_len, tm_prompt, tm_sample, sb_in, sb_attn):
    depth = w_in.shape[0]
    nb, seq, _ = x_prompt.shape
    n_seq, tdec, _ = x_sample.shape
    assert past_len >= POOL_HIST and past_len >= WINDOW
    t_valid = seq + N_META
    tp = -(-t_valid // tm_prompt) * tm_prompt

    meta = jnp.broadcast_to(meta_tokens[None].astype(F32), (nb, N_META, D_MODEL))
    hp = jnp.concatenate(
        [meta, x_prompt, jnp.zeros((nb, tp - t_valid, D_MODEL), F32)], axis=1)
    hs = x_sample.reshape(n_seq * tdec, D_MODEL)

    row3 = lambda a: a.reshape(depth, 1, a.shape[-1])
    w_in_b, pool_w_b, wco_b = w_in.astype(BF16), pool_w.astype(BF16), w_conv_out.astype(BF16)
    wao_b, wo_b = w_attn_out.astype(BF16), w_out.astype(BF16)
    w_up_b, w_down_b = w_up.astype(BF16), w_down.astype(BF16)
    nm3, bin3, ps3 = row3(norm_mix), row3(b_in), row3(pool_scale)
    cb3, cng3, cnb3, nmlp3 = row3(conv_b), row3(conv_norm_g), row3(conv_norm_b), row3(norm_mlp)
    nf2 = norm_final.reshape(1, D_MODEL)

    band = _banded_bias(rel_bias)
    own_only = jnp.asarray(np.arange(2 * WINDOW) >= WINDOW)
    bias_tab = jnp.stack([band, jnp.where(own_only, band, -jnp.inf)])
    bias_tab = bias_tab.reshape(2, N_KV_HEADS, GQA_GROUP, WINDOW, 2 * WINDOW)
    bias_tab = bias_tab.transpose(0, 1, 4, 2, 3).reshape(
        2, N_KV_HEADS, 2 * WINDOW, GQA_GROUP * WINDOW)
    bias_s = band[:, :tdec, :].reshape(N_HEADS * tdec, 2 * WINDOW)
    sink_col = jnp.repeat(attn_sinks.astype(F32), tdec, axis=1).reshape(depth, N_HEADS * tdec, 1)

    cache_k2 = cache_k.reshape(depth, n_seq, WINDOW, KV_DIM)
    cache_v2 = cache_v.reshape(depth, n_seq, WINDOW, KV_DIM)
    sinks = attn_sinks.astype(F32)

    pool_p, pool_s, conv_p, conv_s, k_p, k_s, v_p, v_s = [], [], [], [], [], [], [], []
    for l in range(depth):
        last = l == depth - 1
        hp, pst, cst, kst, vst = _prompt_mixer(
            l, hp, t_valid, tm_prompt, sinks, nm3, w_in_b, bin3, pool_w_b, ps3, conv_w, cb3,
            cng3, cnb3, wco_b, bias_tab, wao_b, wo_b)
        hp = _mlp(l, hp.reshape(nb * tp, D_MODEL), tm_prompt, nmlp3, w_up_b, w_down_b, nf2,
                  last).reshape(nb, tp, D_MODEL)
        pool_p.append(pst[:, POOL_CARRY - POOL_HIST:])
        conv_p.append(cst[:, CONV_CARRY - CONV_HIST:])
        k_p.append(kst.reshape(nb, WINDOW, N_KV_HEADS, HEAD_DIM))
        v_p.append(vst.reshape(nb, WINDOW, N_KV_HEADS, HEAD_DIM))

        mrg, g2, q, kn, vn, po, co = _sample_in(
            l, hs, sb_in, tdec, nm3, w_in_b, bin3, pool_w_b, ps3, conv_w, cb3, cng3, cnb3,
            wco_b, state_pool, state_conv)
        o, ko, vo = _sample_attn(l, q, kn, vn, cache_k2, cache_v2, bias_s, sink_col, sb_attn,
                                 tdec)
        hs = _sample_out(l, hs, mrg, g2, o, tm_sample, wao_b, wo_b, nmlp3, w_up_b, w_down_b,
                         nf2, last)
        pool_s.append(po)
        conv_s.append(co)
        k_s.append(ko.reshape(n_seq, WINDOW, N_KV_HEADS, HEAD_DIM))
        v_s.append(vo.reshape(n_seq, WINDOW, N_KV_HEADS, HEAD_DIM))

    y_prompt = hp[:, N_META:t_valid]
    y_sample = hs.reshape(n_seq, tdec, D_MODEL)
    return (y_prompt, y_sample, jnp.stack(pool_p), jnp.stack(pool_s), jnp.stack(conv_p),
            jnp.stack(conv_s), jnp.stack(k_p), jnp.stack(k_s), jnp.stack(v_p), jnp.stack(v_s))


def kernel(x_prompt, x_sample, state_pool, state_conv, cache_k, cache_v, meta_tokens, rel_bias, norm_mix, w_in, b_in, pool_w, pool_scale, conv_w, conv_b, conv_norm_g, conv_norm_b, w_conv_out, attn_sinks, w_attn_out, w_out, norm_mlp, w_up, w_down, norm_final):
    return _forward(
        x_prompt, x_sample, state_pool, state_conv, cache_k, cache_v, meta_tokens, rel_bias,
        norm_mix, w_in, b_in, pool_w, pool_scale, conv_w, conv_b, conv_norm_g, conv_norm_b,
        w_conv_out, attn_sinks, w_attn_out, w_out, norm_mlp, w_up, w_down, norm_final,
        past_len=8192, tm_prompt=640, tm_sample=512, sb_in=32, sb_attn=16)
```

```python
import functools

import numpy as np
import jax
import jax.numpy as jnp
from jax import lax
from jax.experimental import pallas as pl
from jax.experimental.pallas import tpu as pltpu

F32 = jnp.float32
BF16 = jnp.bfloat16

D_MODEL = 1024
N_META = 16
POOL_WINDOWS = (2, 4, 8, 16)
POOL_WIDTH = 512
POOL_GW = 128
POOL_OUT_GW = 256
POOL_HIST = 15
CONV_DIM = 512
CONV_WIDTH = 31
CONV_HIST = 30
HEAD_DIM = 64
N_HEADS = 16
N_KV_HEADS = 4
GQA_GROUP = 4
KV_DIM = N_KV_HEADS * HEAD_DIM
WINDOW = 128
N_BUCKETS = 32
MAX_DISTANCE = 128
D_FF = 4096
NORM_EPS = 1e-6
C_POOL = 0
C_GLU = C_POOL + POOL_WIDTH
C_Q = C_GLU + 2 * CONV_DIM
C_K = C_Q + N_HEADS * HEAD_DIM
C_V = C_K + KV_DIM
C_GATE = C_V + KV_DIM
D_IN = C_GATE + 3 * D_MODEL

SUBLANES = 8
LANES = 128
POOL_CARRY = 16
CONV_CARRY = 32
FF_CHUNK = 1024
CONV_ROWS = 128
VMEM_LIMIT = 60 * 1024 * 1024


def _dot(a, b):
    return jnp.dot(a, b, preferred_element_type=F32)


def _dot_nt(a, b):
    return lax.dot_general(a, b, (((1,), (1,)), ((), ())), preferred_element_type=F32)


def _rms(x, g):
    ms = jnp.mean(x * x, axis=-1, keepdims=True)
    return x * lax.rsqrt(ms + NORM_EPS) * g


def _layernorm_silu(y, g, b):
    mu = jnp.mean(y, axis=-1, keepdims=True)
    yc = y - mu
    var = jnp.mean(yc * yc, axis=-1, keepdims=True)
    yn = yc * lax.rsqrt(var + NORM_EPS) * g + b
    return yn * jax.nn.sigmoid(yn)


def _softmax_with_sink(logits, sink):
    m = jnp.maximum(jnp.max(logits, axis=-1, keepdims=True), sink)
    p = jnp.exp(logits - m)
    denom = jnp.sum(p, axis=-1, keepdims=True) + jnp.exp(sink - m)
    return p * (1.0 / denom)


def _const_spec(block_shape, index):
    return pl.BlockSpec(block_shape, lambda *_: index, pipeline_mode=pl.Buffered(1))


def _t5_bucket(dist):
    n = np.maximum(dist, 0)
    exact = N_BUCKETS // 2
    large = exact + (np.log(np.maximum(n, 1) / exact) / np.log(MAX_DISTANCE / exact)
                     * (N_BUCKETS - exact)).astype(np.int32)
    large = np.minimum(large, N_BUCKETS - 1)
    return np.where(n < exact, n, large).astype(np.int32)


def _banded_bias(rel_bias):
    n_diag = 3 * WINDOW - 1
    per_dist = rel_bias[_t5_bucket(np.arange(WINDOW + 1))].astype(F32).T
    profile = jnp.full((N_HEADS, n_diag + 1), -jnp.inf, F32)
    profile = profile.at[:, WINDOW - 1:2 * WINDOW].set(per_dist[:, ::-1])
    skew = jnp.tile(profile, (1, WINDOW))[:, :WINDOW * n_diag].reshape(N_HEADS, WINDOW, n_diag)
    return skew[:, :, WINDOW - 1:WINDOW - 1 + 2 * WINDOW]


def _prompt_mixer_body(sinks_ref, h_ref, nm_ref, win_ref, bin_ref, pw_ref, ps_ref, cw_ref,
                       cb_ref, cng_ref, cnb_ref, wco_ref, bias_ref, wao_ref, wo_ref,
                       ho_ref, pst_ref, cst_ref, kst_ref, vst_ref,
                       u_scr, exta, extc, kbuf, vbuf, q_scr, o_scr, s_scr, mrg, xsh, g1_scr, g2_scr,
                       *, layer, tm, e_loc):
    t = pl.program_id(1)

    @pl.when(t == 0)
    def _():
        exta[0:POOL_CARRY, :] = jnp.zeros((POOL_CARRY, POOL_WIDTH), F32)
        extc[0:CONV_CARRY, :] = jnp.zeros((CONV_CARRY, CONV_DIM), F32)
        kbuf[0:WINDOW, :] = jnp.zeros((WINDOW, KV_DIM), BF16)
        vbuf[:, 0:WINDOW] = jnp.zeros((KV_DIM, WINDOW), BF16)

    @pl.when(t > 0)
    def _():
        exta[0:POOL_CARRY, :] = exta[tm:tm + POOL_CARRY, :]
        extc[0:CONV_CARRY, :] = extc[tm:tm + CONV_CARRY, :]
        kbuf[0:WINDOW, :] = kbuf[tm:tm + WINDOW, :]
        vbuf[:, 0:WINDOW] = vbuf[:, tm:tm + WINDOW]

    x = h_ref[...]
    u_scr[...] = _rms(x, nm_ref[...]).astype(BF16)

    def proj(c0, width):
        return _dot(u_scr[...], win_ref[:, c0:c0 + width]) + bin_ref[:, c0:c0 + width]

    def gate(i):
        return jax.nn.sigmoid(proj(C_GATE + i * D_MODEL, D_MODEL))

    c = proj(C_GLU, 2 * CONV_DIM)
    extc[CONV_CARRY:CONV_CARRY + tm, :] = c[:, :CONV_DIM] * jax.nn.sigmoid(c[:, CONV_DIM:])
    a = proj(C_POOL, POOL_WIDTH)
    exta[POOL_CARRY:POOL_CARRY + tm, :] = a

    def project_q():
        q_scr[...] = (proj(C_Q, N_HEADS * HEAD_DIM) * (HEAD_DIM ** -0.5)).astype(BF16)

    def project_kv():
        kv = proj(C_K, 2 * KV_DIM)
        kbuf[WINDOW:WINDOW + tm, :] = kv[:, :KV_DIM].astype(BF16)
        vbuf[:, WINDOW:WINDOW + tm] = kv[:, KV_DIM:].T.astype(BF16)
        kst_ref[...] = kv[e_loc - WINDOW:e_loc, :KV_DIM]
        vst_ref[...] = kv[e_loc - WINDOW:e_loc, KV_DIM:]

    def project_gate(i, dst):
        dst[...] = gate(i)

    pst_ref[...] = exta[e_loc:e_loc + POOL_CARRY, :]
    cst_ref[...] = extc[e_loc:e_loc + CONV_CARRY, :]

    first = CONV_CARRY - CONV_HIST
    last_off = first + CONV_WIDTH - 1

    def conv_chunk(ci):
        r0 = ci * CONV_ROWS
        cols = []
        for li in range(CONV_DIM // LANES):
            c0 = li * LANES
            slot = ((ci * (CONV_DIM // LANES) + li) % 2) * SUBLANES
            acc = jnp.zeros((CONV_ROWS, LANES), F32)
            for sh in range(SUBLANES):
                taps = [a0 for a0 in range(0, last_off + 1, SUBLANES) if first <= a0 + sh <= last_off]
                n = taps[-1] + CONV_ROWS
                if sh:
                    xsh[slot + sh, 0:n, :] = extc[r0 + sh:r0 + sh + n, c0:c0 + LANES]
                for a0 in taps:
                    w_row = cw_ref[a0 + sh - first:a0 + sh - first + 1, c0:c0 + LANES]
                    if sh:
                        acc = acc + xsh[slot + sh, a0:a0 + CONV_ROWS, :] * w_row
                    else:
                        acc = acc + extc[r0 + a0:r0 + a0 + CONV_ROWS, c0:c0 + LANES] * w_row
            cols.append(acc)
        y = jnp.concatenate(cols, axis=1) + cb_ref[...]
        s_scr[r0:r0 + CONV_ROWS, :] = _layernorm_silu(y, cng_ref[...], cnb_ref[...]).astype(BF16)

    def attn_block(qb):
        r0 = qb * WINDOW
        sel = jnp.where(t == 0, 1, 0) if qb == 0 else 0
        outs = []
        for g_kv in range(N_KV_HEADS):
            heads = range(g_kv * GQA_GROUP, (g_kv + 1) * GQA_GROUP)
            q4 = jnp.concatenate(
                [q_scr[r0:r0 + WINDOW, h * HEAD_DIM:(h + 1) * HEAD_DIM] for h in heads], axis=0)
            kw = kbuf[r0:r0 + 2 * WINDOW, g_kv * HEAD_DIM:(g_kv + 1) * HEAD_DIM]
            logits = _dot_nt(kw, q4) + bias_ref[sel, g_kv]
            sink = jnp.concatenate(
                [jnp.full((1, WINDOW), sinks_ref[layer, h], F32) for h in heads], axis=1)
            m = jnp.maximum(jnp.max(logits, axis=0, keepdims=True), sink)
            p = jnp.exp(logits - m)
            denom = jnp.sum(p, axis=0, keepdims=True) + jnp.exp(sink - m)
            vw_t = vbuf[g_kv * HEAD_DIM:(g_kv + 1) * HEAD_DIM, r0:r0 + 2 * WINDOW]
            o_t = _dot(vw_t, p.astype(BF16)) * (1.0 / denom)
            outs.extend(o_t[:, g * WINDOW:(g + 1) * WINDOW] for g in range(GQA_GROUP))
        o_scr[r0:r0 + WINDOW, :] = jnp.concatenate(outs, axis=0).T.astype(BF16)

    def pool_group(gi):
        win = POOL_WINDOWS[gi]
        pos = t * tm + lax.broadcasted_iota(jnp.int32, (tm, 1), 0)
        c0 = gi * POOL_GW
        own = exta[POOL_CARRY:POOL_CARRY + tm, c0:c0 + POOL_GW]
        s = own
        for j in range(1, win):
            s = s + exta[POOL_CARRY - j:POOL_CARRY - j + tm, c0:c0 + POOL_GW]
        cnt = jnp.minimum(pos + 1, win).astype(F32)
        r = (s / cnt - own).astype(BF16)
        o0 = gi * POOL_OUT_GW
        ya = _dot(r, pw_ref[gi]) * ps_ref[:, o0:o0 + POOL_OUT_GW]
        mrg[:, o0:o0 + POOL_OUT_GW] = mrg[:, o0:o0 + POOL_OUT_GW] * ya

    pieces = [project_q, project_kv, functools.partial(project_gate, 0, mrg),
              functools.partial(project_gate, 1, g1_scr), functools.partial(project_gate, 2, g2_scr)]
    for i in range(tm // CONV_ROWS):
        conv_chunk(i)
        if i < len(pieces):
            pieces[i]()
    for piece in pieces[tm // CONV_ROWS:]:
        piece()
    for gi in range(len(POOL_WINDOWS)):
        pool_group(gi)
    for i in range(tm // WINDOW):
        attn_block(i)

    yb = _dot(s_scr[...], wco_ref[...])
    mrg[...] = mrg[...] + g1_scr[...] * yb
    yc = _dot(o_scr[...], wao_ref[...])
    merged = mrg[...] + g2_scr[...] * yc
    ho_ref[...] = x + _dot(merged.astype(BF16), wo_ref[...])


def _prompt_mixer(layer, hp, t_valid, tm, sinks, norm_mix, w_in, b_in, pool_w, pool_scale,
                  conv_w, conv_b, conv_ng, conv_nb, w_conv_out, bias_tab, w_attn_out, w_out):
    nb, tp, _ = hp.shape
    n_tiles = tp // tm
    last_tile = (t_valid - 1) // tm
    e_loc = t_valid - last_tile * tm
    assert last_tile == n_tiles - 1 and e_loc >= WINDOW and e_loc % SUBLANES == 0
    body = functools.partial(_prompt_mixer_body, layer=layer, tm=tm, e_loc=e_loc)
    lsel3 = (layer, 0, 0)
    in_specs = [
        pl.BlockSpec(memory_space=pltpu.SMEM),
        pl.BlockSpec((None, tm, D_MODEL), lambda b, t: (b, t, 0)),
        _const_spec((None, 1, D_MODEL), lsel3),
        _const_spec((None, D_MODEL, D_IN), lsel3),
        _const_spec((None, 1, D_IN), lsel3),
        _const_spec((None, len(POOL_WINDOWS), POOL_GW, POOL_OUT_GW), (layer, 0, 0, 0)),
        _const_spec((None, 1, D_MODEL), lsel3),
        _const_spec((None, CONV_WIDTH, CONV_DIM), lsel3),
        _const_spec((None, 1, CONV_DIM), lsel3),
        _const_spec((None, 1, CONV_DIM), lsel3),
        _const_spec((None, 1, CONV_DIM), lsel3),
        _const_spec((None, CONV_DIM, D_MODEL), lsel3),
        _const_spec((2, N_KV_HEADS, 2 * WINDOW, GQA_GROUP * WINDOW), (0, 0, 0, 0)),
        _const_spec((None, D_MODEL, D_MODEL), lsel3),
        _const_spec((None, D_MODEL, D_MODEL), lsel3),
    ]
    out_shape = [
        jax.ShapeDtypeStruct((nb, tp, D_MODEL), F32),
        jax.ShapeDtypeStruct((nb, POOL_CARRY, POOL_WIDTH), F32),
        jax.ShapeDtypeStruct((nb, CONV_CARRY, CONV_DIM), F32),
        jax.ShapeDtypeStruct((nb, WINDOW, KV_DIM), F32),
        jax.ShapeDtypeStruct((nb, WINDOW, KV_DIM), F32),
    ]
    out_specs = [
        pl.BlockSpec((None, tm, D_MODEL), lambda b, t: (b, t, 0)),
        pl.BlockSpec((None, POOL_CARRY, POOL_WIDTH), lambda b, t: (b, 0, 0)),
        pl.BlockSpec((None, CONV_CARRY, CONV_DIM), lambda b, t: (b, 0, 0)),
        pl.BlockSpec((None, WINDOW, KV_DIM), lambda b, t: (b, 0, 0)),
        pl.BlockSpec((None, WINDOW, KV_DIM), lambda b, t: (b, 0, 0)),
    ]
    scratch = [
        pltpu.VMEM((tm, D_MODEL), BF16),
        pltpu.VMEM((POOL_CARRY + tm, POOL_WIDTH), F32),
        pltpu.VMEM((CONV_CARRY + tm, CONV_DIM), F32),
        pltpu.VMEM((WINDOW + tm, KV_DIM), BF16),
        pltpu.VMEM((KV_DIM, WINDOW + tm), BF16),
        pltpu.VMEM((tm, D_MODEL), BF16),
        pltpu.VMEM((tm, D_MODEL), BF16),
        pltpu.VMEM((tm, CONV_DIM), BF16),
        pltpu.VMEM((tm, D_MODEL), F32),
        pltpu.VMEM((2 * SUBLANES, CONV_ROWS + CONV_CARRY, LANES), F32),
        pltpu.VMEM((tm, D_MODEL), F32),
        pltpu.VMEM((tm, D_MODEL), F32),
    ]
    return pl.pallas_call(
        body,
        grid=(nb, n_tiles),
        in_specs=in_specs,
        out_specs=out_specs,
        out_shape=out_shape,
        scratch_shapes=scratch,
        compiler_params=pltpu.CompilerParams(
            dimension_semantics=("arbitrary", "arbitrary"), vmem_limit_bytes=VMEM_LIMIT),
        name=f"prompt_mixer_l{layer}",
    )(sinks, hp, norm_mix, w_in, b_in, pool_w, pool_scale, conv_w, conv_b, conv_ng, conv_nb,
      w_conv_out, bias_tab, w_attn_out, w_out)


def _mlp_rows(x, g_ref, wup_ref, wdn_ref):
    u = _rms(x, g_ref[...]).astype(BF16)
    acc = x
    for c0 in range(0, D_FF, FF_CHUNK):
        up = jnp.maximum(_dot(u, wup_ref[:, c0:c0 + FF_CHUNK]), 0.0)
        acc = acc + _dot((up * up).astype(BF16), wdn_ref[c0:c0 + FF_CHUNK, :])
    return acc


def _mlp_body(h_ref, g_ref, wup_ref, wdn_ref, gf_ref, o_ref, *, final_norm):
    y = _mlp_rows(h_ref[...], g_ref, wup_ref, wdn_ref)
    if final_norm:
        y = _rms(y, gf_ref[...])
    o_ref[...] = y


def _mlp(layer, h2d, tm, norm_mlp, w_up, w_down, norm_final, final_norm):
    rows = h2d.shape[0]
    lsel3 = (layer, 0, 0)
    return pl.pallas_call(
        functools.partial(_mlp_body, final_norm=final_norm),
        grid=(rows // tm,),
        in_specs=[
            pl.BlockSpec((tm, D_MODEL), lambda i: (i, 0)),
            _const_spec((None, 1, D_MODEL), lsel3),
            _const_spec((None, D_MODEL, D_FF), lsel3),
            _const_spec((None, D_FF, D_MODEL), lsel3),
            _const_spec((1, D_MODEL), (0, 0)),
        ],
        out_specs=pl.BlockSpec((tm, D_MODEL), lambda i: (i, 0)),
        out_shape=jax.ShapeDtypeStruct((rows, D_MODEL), F32),
        compiler_params=pltpu.CompilerParams(
            dimension_semantics=("arbitrary",), vmem_limit_bytes=VMEM_LIMIT),
        name=f"mlp_l{layer}",
    )(h2d, norm_mlp, w_up, w_down, norm_final)


def _sample_in_body(h_ref, nm_ref, win_ref, bin_ref, pw_ref, ps_ref, cw_ref, cb_ref, cng_ref,
                    cnb_ref, wco_ref, sp_ref, sc_ref, *rest, sb, tdec):
    mrg_ref, g2_ref, q_ref, kn_ref, vn_ref, po_ref, co_ref, exta, extc = rest[-9:]
    rows = sb * tdec
    x = h_ref[...]
    u = _rms(x, nm_ref[...]).astype(BF16)

    def proj(c0, width):
        return _dot(u, win_ref[:, c0:c0 + width]) + bin_ref[:, c0:c0 + width]

    def gate(i):
        return jax.nn.sigmoid(proj(C_GATE + i * D_MODEL, D_MODEL))

    a = proj(C_POOL, POOL_WIDTH)
    exta[:, 1:POOL_CARRY, :] = sp_ref[...]
    exta[:, POOL_CARRY:POOL_CARRY + tdec, :] = a.reshape(sb, tdec, POOL_WIDTH)
    ya_parts = []
    for gi, win in enumerate(POOL_WINDOWS):
        c0 = gi * POOL_GW
        own = a[:, c0:c0 + POOL_GW]
        s = own
        for j in range(1, win):
            s = s + exta[:, POOL_CARRY - j:POOL_CARRY - j + tdec, c0:c0 + POOL_GW].reshape(
                rows, POOL_GW)
        r = (s / float(win) - own).astype(BF16)
        ya_parts.append(_dot(r, pw_ref[gi]))
    ya = jnp.concatenate(ya_parts, axis=1) * ps_ref[...]
    po_ref[...] = exta[:, POOL_CARRY + tdec - POOL_HIST:POOL_CARRY + tdec, :]
    merged = gate(0) * ya

    c = proj(C_GLU, 2 * CONV_DIM)
    glu = c[:, :CONV_DIM] * jax.nn.sigmoid(c[:, CONV_DIM:])
    first = CONV_CARRY - CONV_HIST
    extc[:, first:CONV_CARRY, :] = sc_ref[...]
    extc[:, CONV_CARRY:CONV_CARRY + tdec, :] = glu.reshape(sb, tdec, CONV_DIM)
    acc = jnp.zeros((rows, CONV_DIM), F32)
    for j in range(CONV_WIDTH):
        acc = acc + extc[:, first + j:first + j + tdec, :].reshape(rows, CONV_DIM) * cw_ref[j:j + 1, :]
    y = acc + cb_ref[...]
    s_act = _layernorm_silu(y, cng_ref[...], cnb_ref[...]).astype(BF16)
    co_ref[...] = extc[:, CONV_CARRY + tdec - CONV_HIST:CONV_CARRY + tdec, :]
    merged = merged + gate(1) * _dot(s_act, wco_ref[...])

    mrg_ref[...] = merged
    g2_ref[...] = gate(2)
    q_ref[...] = proj(C_Q, N_HEADS * HEAD_DIM) * (HEAD_DIM ** -0.5)
    kv = proj(C_K, 2 * KV_DIM)
    kn_ref[...] = kv[:, :KV_DIM]
    vn_ref[...] = kv[:, KV_DIM:]


def _sample_in(layer, hs, sb, tdec, norm_mix, w_in, b_in, pool_w, pool_scale, conv_w, conv_b,
               conv_ng, conv_nb, w_conv_out, state_pool, state_conv, pool_all, conv_all):
    depth = state_pool.shape[0]
    rows_all = hs.shape[0]
    n_seq = rows_all // tdec
    rows = sb * tdec
    lsel3 = (layer, 0, 0)
    row_spec = lambda w: pl.BlockSpec((rows, w), lambda i: (i, 0))
    in_specs = [
        row_spec(D_MODEL),
        _const_spec((None, 1, D_MODEL), lsel3),
        _const_spec((None, D_MODEL, D_IN), lsel3),
        _const_spec((None, 1, D_IN), lsel3),
        _const_spec((None, len(POOL_WINDOWS), POOL_GW, POOL_OUT_GW), (layer, 0, 0, 0)),
        _const_spec((None, 1, D_MODEL), lsel3),
        _const_spec((None, CONV_WIDTH, CONV_DIM), lsel3),
        _const_spec((None, 1, CONV_DIM), lsel3),
        _const_spec((None, 1, CONV_DIM), lsel3),
        _const_spec((None, 1, CONV_DIM), lsel3),
        _const_spec((None, CONV_DIM, D_MODEL), lsel3),
        pl.BlockSpec((None, sb, POOL_HIST, POOL_WIDTH), lambda i: (layer, i, 0, 0)),
        pl.BlockSpec((None, sb, CONV_HIST, CONV_DIM), lambda i: (layer, i, 0, 0)),
    ]
    out_shape = [
        jax.ShapeDtypeStruct((rows_all, D_MODEL), F32),
        jax.ShapeDtypeStruct((rows_all, D_MODEL), F32),
        jax.ShapeDtypeStruct((rows_all, D_MODEL), F32),
        jax.ShapeDtypeStruct((rows_all, KV_DIM), F32),
        jax.ShapeDtypeStruct((rows_all, KV_DIM), F32),
        jax.ShapeDtypeStruct((depth, n_seq, POOL_HIST, POOL_WIDTH), F32),
        jax.ShapeDtypeStruct((depth, n_seq, CONV_HIST, CONV_DIM), F32),
    ]
    out_specs = [
        row_spec(D_MODEL), row_spec(D_MODEL), row_spec(D_MODEL), row_spec(KV_DIM), row_spec(KV_DIM),
        pl.BlockSpec((None, sb, POOL_HIST, POOL_WIDTH), lambda i: (layer, i, 0, 0)),
        pl.BlockSpec((None, sb, CONV_HIST, CONV_DIM), lambda i: (layer, i, 0, 0)),
    ]
    scratch = [
        pltpu.VMEM((sb, POOL_CARRY + tdec, POOL_WIDTH), F32),
        pltpu.VMEM((sb, CONV_CARRY + tdec, CONV_DIM), F32),
    ]
    args = [hs, norm_mix, w_in, b_in, pool_w, pool_scale, conv_w, conv_b, conv_ng, conv_nb,
            w_conv_out, state_pool, state_conv]
    extra_specs, extra_args, aliases = _alias_tail((pool_all, conv_all), len(args), 5)
    return pl.pallas_call(
        functools.partial(_sample_in_body, sb=sb, tdec=tdec),
        grid=(n_seq // sb,),
        in_specs=in_specs + extra_specs,
        out_specs=out_specs,
        out_shape=out_shape,
        input_output_aliases=aliases,
        scratch_shapes=scratch,
        compiler_params=pltpu.CompilerParams(
            dimension_semantics=("arbitrary",), vmem_limit_bytes=VMEM_LIMIT),
        name=f"sample_in_l{layer}",
    )(*(args + extra_args))


def _sample_attn_body(q_ref, kn_ref, vn_ref, ck_ref, cv_ref, bias_ref, sink_ref, *rest,
                      sb, tdec, unroll):
    o_ref, ko_ref, vo_ref, qbd, kx, vx = rest[-6:]
    n_keys = WINDOW + tdec

    @pl.when(pl.program_id(0) == 0)
    def _():
        qbd[...] = jnp.zeros(qbd.shape, F32)
        kx[...] = jnp.zeros(kx.shape, F32)
        vx[...] = jnp.zeros(vx.shape, F32)

    def one_sequence(s, slot):
        r0 = pl.multiple_of(s * tdec, tdec)
        qs = q_ref[pl.ds(r0, tdec), :]
        kn = kn_ref[pl.ds(r0, tdec), :]
        vn = vn_ref[pl.ds(r0, tdec), :]
        kc = ck_ref[s]
        vc = cv_ref[s]
        for h in range(N_HEADS):
            g_kv = h // GQA_GROUP
            qbd[slot, h * tdec:(h + 1) * tdec, g_kv * HEAD_DIM:(g_kv + 1) * HEAD_DIM] = (
                qs[:, h * HEAD_DIM:(h + 1) * HEAD_DIM])
        kx[slot, 0:WINDOW, :] = kc
        kx[slot, WINDOW:n_keys, :] = kn
        vx[slot, 0:WINDOW, :] = vc
        vx[slot, WINDOW:n_keys, :] = vn
        logits = _dot_nt(qbd[slot].astype(BF16), kx[slot].astype(BF16)) + bias_ref[...]
        p = _softmax_with_sink(logits, sink_ref[...]).astype(BF16)
        o_full = _dot(p, vx[slot].astype(BF16))
        for h in range(N_HEADS):
            g_kv = h // GQA_GROUP
            o_ref[pl.ds(r0, tdec), h * HEAD_DIM:(h + 1) * HEAD_DIM] = (
                o_full[h * tdec:(h + 1) * tdec, g_kv * HEAD_DIM:(g_kv + 1) * HEAD_DIM])
        ko_ref[s, 0:WINDOW - tdec, :] = kc[tdec:, :]
        ko_ref[s, WINDOW - tdec:WINDOW, :] = kn
        vo_ref[s, 0:WINDOW - tdec, :] = vc[tdec:, :]
        vo_ref[s, WINDOW - tdec:WINDOW, :] = vn

    def several_sequences(i, carry):
        for slot in range(unroll):
            one_sequence(i * unroll + slot, slot)
        return carry

    lax.fori_loop(0, sb // unroll, several_sequences, 0)


def _alias_tail(stacked, n_args, first_out):
    if stacked[0] is None:
        return [], [], {}
    specs = [pl.BlockSpec(memory_space=pl.ANY) for _ in stacked]
    return specs, list(stacked), {n_args + i: first_out + i for i in range(len(stacked))}


def _sample_attn(layer, q, kn, vn, cache_k, cache_v, bias_s, sink_col, k_all, v_all, sb, tdec):
    depth = cache_k.shape[0]
    rows_all = q.shape[0]
    n_seq = rows_all // tdec
    rows = sb * tdec
    key_pad = 2 * WINDOW
    unroll = 4 if sb % 4 == 0 else 1
    row_spec = lambda w: pl.BlockSpec((rows, w), lambda i: (i, 0))
    cache_spec = pl.BlockSpec((None, sb, WINDOW, KV_DIM), lambda i: (layer, i, 0, 0))
    in_specs = [
        row_spec(D_MODEL), row_spec(KV_DIM), row_spec(KV_DIM), cache_spec, cache_spec,
        _const_spec((N_HEADS * tdec, key_pad), (0, 0)),
        _const_spec((None, N_HEADS * tdec, 1), (layer, 0, 0)),
    ]
    args = [q, kn, vn, cache_k, cache_v, bias_s, sink_col]
    extra_specs, extra_args, aliases = _alias_tail((k_all, v_all), len(args), 1)
    stacked = jax.ShapeDtypeStruct((depth, n_seq, WINDOW, KV_DIM), F32)
    return pl.pallas_call(
        functools.partial(_sample_attn_body, sb=sb, tdec=tdec, unroll=unroll),
        grid=(n_seq // sb,),
        in_specs=in_specs + extra_specs,
        out_specs=[row_spec(D_MODEL), cache_spec, cache_spec],
        out_shape=[jax.ShapeDtypeStruct((rows_all, D_MODEL), F32), stacked, stacked],
        input_output_aliases=aliases,
        scratch_shapes=[
            pltpu.VMEM((unroll, N_HEADS * tdec, KV_DIM), F32),
            pltpu.VMEM((unroll, key_pad, KV_DIM), F32),
            pltpu.VMEM((unroll, key_pad, KV_DIM), F32),
        ],
        compiler_params=pltpu.CompilerParams(
            dimension_semantics=("arbitrary",), vmem_limit_bytes=VMEM_LIMIT),
        name=f"sample_attn_l{layer}",
    )(*(args + extra_args))


def _sample_out_body(h_ref, mrg_ref, g2_ref, o_ref, wao_ref, wo_ref, g_ref, wup_ref, wdn_ref,
                     gf_ref, out_ref, *, final_norm):
    yc = _dot(o_ref[...].astype(BF16), wao_ref[...])
    merged = mrg_ref[...] + g2_ref[...] * yc
    h1 = h_ref[...] + _dot(merged.astype(BF16), wo_ref[...])
    y = _mlp_rows(h1, g_ref, wup_ref, wdn_ref)
    if final_norm:
        y = _rms(y, gf_ref[...])
    out_ref[...] = y


def _sample_out(layer, hs, mrg, g2, o, tm, w_attn_out, w_out, norm_mlp, w_up, w_down,
                norm_final, final_norm):
    rows = hs.shape[0]
    lsel3 = (layer, 0, 0)
    row_spec = pl.BlockSpec((tm, D_MODEL), lambda i: (i, 0))
    return pl.pallas_call(
        functools.partial(_sample_out_body, final_norm=final_norm),
        grid=(rows // tm,),
        in_specs=[
            row_spec, row_spec, row_spec, row_spec,
            _const_spec((None, D_MODEL, D_MODEL), lsel3),
            _const_spec((None, D_MODEL, D_MODEL), lsel3),
            _const_spec((None, 1, D_MODEL), lsel3),
            _const_spec((None, D_MODEL, D_FF), lsel3),
            _const_spec((None, D_FF, D_MODEL), lsel3),
            _const_spec((1, D_MODEL), (0, 0)),
        ],
        out_specs=row_spec,
        out_shape=jax.ShapeDtypeStruct((rows, D_MODEL), F32),
        compiler_params=pltpu.CompilerParams(
            dimension_semantics=("arbitrary",), vmem_limit_bytes=VMEM_LIMIT),
        name=f"sample_out_l{layer}",
    )(hs, mrg, g2, o, w_attn_out, w_out, norm_mlp, w_up, w_down, norm_final)


def _forward(x_prompt, x_sample, state_pool, state_conv, cache_k, cache_v, meta_tokens,
             rel_bias, norm_mix, w_in, b_in, pool_w, pool_scale, conv_w, conv_b, conv_norm_g,
             conv_norm_b, w_conv_out, attn_sinks, w_attn_out, w_out, norm_mlp, w_up, w_down,
             norm_final, *, past_len, tm_prompt, tm_sample, sb_in, sb_attn):
    depth = w_in.shape[0]
    nb, seq, _ = x_prompt.shape
    n_seq, tdec, _ = x_sample.shape
    assert past_len >= POOL_HIST and past_len >= WINDOW
    t_valid = seq + N_META
    tp = -(-t_valid // tm_prompt) * tm_prompt

    meta = jnp.broadcast_to(meta_tokens[None].astype(F32), (nb, N_META, D_MODEL))
    hp = jnp.concatenate(
        [meta, x_prompt, jnp.zeros((nb, tp - t_valid, D_MODEL), F32)], axis=1)
    hs = x_sample.reshape(n_seq * tdec, D_MODEL)

    row3 = lambda a: a.reshape(depth, 1, a.shape[-1])
    w_in_b, pool_w_b, wco_b = w_in.astype(BF16), pool_w.astype(BF16), w_conv_out.astype(BF16)
    wao_b, wo_b = w_attn_out.astype(BF16), w_out.astype(BF16)
    w_up_b, w_down_b = w_up.astype(BF16), w_down.astype(BF16)
    nm3, bin3, ps3 = row3(norm_mix), row3(b_in), row3(pool_scale)
    cb3, cng3, cnb3, nmlp3 = row3(conv_b), row3(conv_norm_g), row3(conv_norm_b), row3(norm_mlp)
    nf2 = norm_final.reshape(1, D_MODEL)

    band = _banded_bias(rel_bias)
    own_only = jnp.asarray(np.arange(2 * WINDOW) >= WINDOW)
    bias_tab = jnp.stack([band, jnp.where(own_only, band, -jnp.inf)])
    bias_tab = bias_tab.reshape(2, N_KV_HEADS, GQA_GROUP, WINDOW, 2 * WINDOW)
    bias_tab = bias_tab.transpose(0, 1, 4, 2, 3).reshape(
        2, N_KV_HEADS, 2 * WINDOW, GQA_GROUP * WINDOW)
    bias_s = band[:, :tdec, :].reshape(N_HEADS * tdec, 2 * WINDOW)
    sink_col = jnp.repeat(attn_sinks.astype(F32), tdec, axis=1).reshape(depth, N_HEADS * tdec, 1)

    cache_k2 = cache_k.reshape(depth, n_seq, WINDOW, KV_DIM)
    cache_v2 = cache_v.reshape(depth, n_seq, WINDOW, KV_DIM)
    sinks = attn_sinks.astype(F32)

    pool_p, conv_p, k_p, v_p = [], [], [], []
    pool_s = conv_s = k_s = v_s = None
    for l in range(depth):
        last = l == depth - 1
        hp, pst, cst, kst, vst = _prompt_mixer(
            l, hp, t_valid, tm_prompt, sinks, nm3, w_in_b, bin3, pool_w_b, ps3, conv_w, cb3,
            cng3, cnb3, wco_b, bias_tab, wao_b, wo_b)
        hp = _mlp(l, hp.reshape(nb * tp, D_MODEL), tm_prompt, nmlp3, w_up_b, w_down_b, nf2,
                  last).reshape(nb, tp, D_MODEL)
        pool_p.append(pst[:, POOL_CARRY - POOL_HIST:])
        conv_p.append(cst[:, CONV_CARRY - CONV_HIST:])
        k_p.append(kst.reshape(nb, WINDOW, N_KV_HEADS, HEAD_DIM))
        v_p.append(vst.reshape(nb, WINDOW, N_KV_HEADS, HEAD_DIM))

        mrg, g2, q, kn, vn, pool_s, conv_s = _sample_in(
            l, hs, sb_in, tdec, nm3, w_in_b, bin3, pool_w_b, ps3, conv_w, cb3, cng3, cnb3,
            wco_b, state_pool, state_conv, pool_s, conv_s)
        o, k_s, v_s = _sample_attn(l, q, kn, vn, cache_k2, cache_v2, bias_s, sink_col, k_s, v_s,
                                   sb_attn, tdec)
        hs = _sample_out(l, hs, mrg, g2, o, tm_sample, wao_b, wo_b, nmlp3, w_up_b, w_down_b,
                         nf2, last)

    y_prompt = hp[:, N_META:t_valid]
    y_sample = hs.reshape(n_seq, tdec, D_MODEL)
    kv_shape = (depth, n_seq, WINDOW, N_KV_HEADS, HEAD_DIM)
    return (y_prompt, y_sample, jnp.stack(pool_p), pool_s, jnp.stack(conv_p), conv_s,
            jnp.stack(k_p), k_s.reshape(kv_shape), jnp.stack(v_p), v_s.reshape(kv_shape))


def kernel(x_prompt, x_sample, state_pool, state_conv, cache_k, cache_v, meta_tokens, rel_bias, norm_mix, w_in, b_in, pool_w, pool_scale, conv_w, conv_b, conv_norm_g, conv_norm_b, w_conv_out, attn_sinks, w_attn_out, w_out, norm_mlp, w_up, w_down, norm_final):
    return _forward(
        x_prompt, x_sample, state_pool, state_conv, cache_k, cache_v, meta_tokens, rel_bias,
        norm_mix, w_in, b_in, pool_w, pool_scale, conv_w, conv_b, conv_norm_g, conv_norm_b,
        w_conv_out, attn_sinks, w_attn_out, w_out, norm_mlp, w_up, w_down, norm_final,
        past_len=8192, tm_prompt=640, tm_sample=512, sb_in=32, sb_attn=16)
```

```python
import functools

import numpy as np
import jax
import jax.numpy as jnp
from jax import lax
from jax.experimental import pallas as pl
from jax.experimental.pallas import tpu as pltpu

F32 = jnp.float32
BF16 = jnp.bfloat16

D_MODEL = 1024
N_META = 16
POOL_WINDOWS = (2, 4, 8, 16)
POOL_WIDTH = 512
POOL_GW = 128
POOL_OUT_GW = 256
POOL_HIST = 15
CONV_DIM = 512
CONV_WIDTH = 31
CONV_HIST = 30
HEAD_DIM = 64
N_HEADS = 16
N_KV_HEADS = 4
GQA_GROUP = 4
KV_DIM = N_KV_HEADS * HEAD_DIM
WINDOW = 128
N_BUCKETS = 32
MAX_DISTANCE = 128
D_FF = 4096
NORM_EPS = 1e-6
C_POOL = 0
C_GLU = C_POOL + POOL_WIDTH
C_Q = C_GLU + 2 * CONV_DIM
C_K = C_Q + N_HEADS * HEAD_DIM
C_V = C_K + KV_DIM
C_GATE = C_V + KV_DIM
D_IN = C_GATE + 3 * D_MODEL

SUBLANES = 8
LANES = 128
POOL_CARRY = 16
CONV_CARRY = 32
FF_CHUNK = 1024
CONV_ROWS = 128
VMEM_LIMIT = 60 * 1024 * 1024


def _dot(a, b):
    return jnp.dot(a, b, preferred_element_type=F32)


def _dot_nt(a, b):
    return lax.dot_general(a, b, (((1,), (1,)), ((), ())), preferred_element_type=F32)


def _rms(x, g):
    ms = jnp.mean(x * x, axis=-1, keepdims=True)
    return x * lax.rsqrt(ms + NORM_EPS) * g


def _layernorm_silu(y, g, b):
    mu = jnp.mean(y, axis=-1, keepdims=True)
    yc = y - mu
    var = jnp.mean(yc * yc, axis=-1, keepdims=True)
    yn = yc * lax.rsqrt(var + NORM_EPS) * g + b
    return yn * jax.nn.sigmoid(yn)


def _softmax_with_sink(logits, sink):
    m = jnp.maximum(jnp.max(logits, axis=-1, keepdims=True), sink)
    p = jnp.exp(logits - m)
    denom = jnp.sum(p, axis=-1, keepdims=True) + jnp.exp(sink - m)
    return p * (1.0 / denom)


def _const_spec(block_shape, index):
    return pl.BlockSpec(block_shape, lambda *_: index, pipeline_mode=pl.Buffered(1))


def _t5_bucket(dist):
    n = np.maximum(dist, 0)
    exact = N_BUCKETS // 2
    large = exact + (np.log(np.maximum(n, 1) / exact) / np.log(MAX_DISTANCE / exact)
                     * (N_BUCKETS - exact)).astype(np.int32)
    large = np.minimum(large, N_BUCKETS - 1)
    return np.where(n < exact, n, large).astype(np.int32)


def _banded_bias(rel_bias):
    n_diag = 3 * WINDOW - 1
    per_dist = rel_bias[_t5_bucket(np.arange(WINDOW + 1))].astype(F32).T
    profile = jnp.full((N_HEADS, n_diag + 1), -jnp.inf, F32)
    profile = profile.at[:, WINDOW - 1:2 * WINDOW].set(per_dist[:, ::-1])
    skew = jnp.tile(profile, (1, WINDOW))[:, :WINDOW * n_diag].reshape(N_HEADS, WINDOW, n_diag)
    return skew[:, :, WINDOW - 1:WINDOW - 1 + 2 * WINDOW]


def _prompt_mixer_body(sinks_ref, h_ref, nm_ref, win_ref, bin_ref, pw_ref, ps_ref, cw_ref,
                       cb_ref, cng_ref, cnb_ref, wco_ref, bias_ref, wao_ref, wo_ref,
                       ho_ref, pst_ref, cst_ref, kst_ref, vst_ref,
                       u_scr, exta, extc, kbuf, vbuf, q_scr, o_scr, s_scr, mrg, xsh, g1_scr, g2_scr,
                       *, layer, tm, e_loc):
    t = pl.program_id(1)

    @pl.when(t == 0)
    def _():
        exta[0:POOL_CARRY, :] = jnp.zeros((POOL_CARRY, POOL_WIDTH), F32)
        extc[0:CONV_CARRY, :] = jnp.zeros((CONV_CARRY, CONV_DIM), F32)
        kbuf[0:WINDOW, :] = jnp.zeros((WINDOW, KV_DIM), BF16)
        vbuf[:, 0:WINDOW] = jnp.zeros((KV_DIM, WINDOW), BF16)

    @pl.when(t > 0)
    def _():
        exta[0:POOL_CARRY, :] = exta[tm:tm + POOL_CARRY, :]
        extc[0:CONV_CARRY, :] = extc[tm:tm + CONV_CARRY, :]
        kbuf[0:WINDOW, :] = kbuf[tm:tm + WINDOW, :]
        vbuf[:, 0:WINDOW] = vbuf[:, tm:tm + WINDOW]

    x = h_ref[...]
    u_scr[...] = _rms(x, nm_ref[...]).astype(BF16)

    def proj(c0, width):
        return _dot(u_scr[...], win_ref[:, c0:c0 + width]) + bin_ref[:, c0:c0 + width]

    def gate(i):
        return jax.nn.sigmoid(proj(C_GATE + i * D_MODEL, D_MODEL))

    c = proj(C_GLU, 2 * CONV_DIM)
    extc[CONV_CARRY:CONV_CARRY + tm, :] = c[:, :CONV_DIM] * jax.nn.sigmoid(c[:, CONV_DIM:])
    a = proj(C_POOL, POOL_WIDTH)
    exta[POOL_CARRY:POOL_CARRY + tm, :] = a

    def project_q():
        q_scr[...] = (proj(C_Q, N_HEADS * HEAD_DIM) * (HEAD_DIM ** -0.5)).astype(BF16)

    def project_kv():
        kv = proj(C_K, 2 * KV_DIM)
        kbuf[WINDOW:WINDOW + tm, :] = kv[:, :KV_DIM].astype(BF16)
        vbuf[:, WINDOW:WINDOW + tm] = kv[:, KV_DIM:].T.astype(BF16)
        kst_ref[...] = kv[e_loc - WINDOW:e_loc, :KV_DIM]
        vst_ref[...] = kv[e_loc - WINDOW:e_loc, KV_DIM:]

    def project_gate(i, dst):
        dst[...] = gate(i)

    pst_ref[...] = exta[e_loc:e_loc + POOL_CARRY, :]
    cst_ref[...] = extc[e_loc:e_loc + CONV_CARRY, :]

    first = CONV_CARRY - CONV_HIST
    last_off = first + CONV_WIDTH - 1

    def conv_chunk(ci):
        r0 = ci * CONV_ROWS
        cols = []
        for li in range(CONV_DIM // LANES):
            c0 = li * LANES
            slot = ((ci * (CONV_DIM // LANES) + li) % 2) * SUBLANES
            acc = jnp.zeros((CONV_ROWS, LANES), F32)
            for sh in range(SUBLANES):
                taps = [a0 for a0 in range(0, last_off + 1, SUBLANES) if first <= a0 + sh <= last_off]
                n = taps[-1] + CONV_ROWS
                if sh:
                    xsh[slot + sh, 0:n, :] = extc[r0 + sh:r0 + sh + n, c0:c0 + LANES]
                for a0 in taps:
                    w_row = cw_ref[a0 + sh - first:a0 + sh - first + 1, c0:c0 + LANES]
                    if sh:
                        acc = acc + xsh[slot + sh, a0:a0 + CONV_ROWS, :] * w_row
                    else:
                        acc = acc + extc[r0 + a0:r0 + a0 + CONV_ROWS, c0:c0 + LANES] * w_row
            cols.append(acc)
        y = jnp.concatenate(cols, axis=1) + cb_ref[...]
        s_scr[r0:r0 + CONV_ROWS, :] = _layernorm_silu(y, cng_ref[...], cnb_ref[...]).astype(BF16)

    def attn_qk(qb):
        r0 = qb * WINDOW
        raw = []
        for g_kv in range(N_KV_HEADS):
            heads = range(g_kv * GQA_GROUP, (g_kv + 1) * GQA_GROUP)
            q4 = jnp.concatenate(
                [q_scr[r0:r0 + WINDOW, h * HEAD_DIM:(h + 1) * HEAD_DIM] for h in heads], axis=0)
            kw = kbuf[r0:r0 + 2 * WINDOW, g_kv * HEAD_DIM:(g_kv + 1) * HEAD_DIM]
            raw.append(_dot_nt(kw, q4))
        return raw

    def attn_softmax(qb, raw):
        sel = jnp.where(t == 0, 1, 0) if qb == 0 else 0
        probs = []
        for g_kv in range(N_KV_HEADS):
            heads = range(g_kv * GQA_GROUP, (g_kv + 1) * GQA_GROUP)
            logits = raw[g_kv] + bias_ref[sel, g_kv]
            sink = jnp.concatenate(
                [jnp.full((1, WINDOW), sinks_ref[layer, h], F32) for h in heads], axis=1)
            m = jnp.maximum(jnp.max(logits, axis=0, keepdims=True), sink)
            p = jnp.exp(logits - m)
            denom = jnp.sum(p, axis=0, keepdims=True) + jnp.exp(sink - m)
            probs.append((p.astype(BF16), 1.0 / denom))
        return probs

    def attn_pv(qb, probs):
        r0 = qb * WINDOW
        outs = []
        for g_kv in range(N_KV_HEADS):
            p, inv_denom = probs[g_kv]
            vw_t = vbuf[g_kv * HEAD_DIM:(g_kv + 1) * HEAD_DIM, r0:r0 + 2 * WINDOW]
            o_t = _dot(vw_t, p) * inv_denom
            outs.extend(o_t[:, g * WINDOW:(g + 1) * WINDOW] for g in range(GQA_GROUP))
        o_scr[r0:r0 + WINDOW, :] = jnp.concatenate(outs, axis=0).T.astype(BF16)

    def pool_group(gi):
        win = POOL_WINDOWS[gi]
        pos = t * tm + lax.broadcasted_iota(jnp.int32, (tm, 1), 0)
        c0 = gi * POOL_GW
        own = exta[POOL_CARRY:POOL_CARRY + tm, c0:c0 + POOL_GW]
        s = own
        for j in range(1, win):
            s = s + exta[POOL_CARRY - j:POOL_CARRY - j + tm, c0:c0 + POOL_GW]
        cnt = jnp.minimum(pos + 1, win).astype(F32)
        r = (s / cnt - own).astype(BF16)
        o0 = gi * POOL_OUT_GW
        ya = _dot(r, pw_ref[gi]) * ps_ref[:, o0:o0 + POOL_OUT_GW]
        mrg[:, o0:o0 + POOL_OUT_GW] = mrg[:, o0:o0 + POOL_OUT_GW] * ya

    vector_jobs = [functools.partial(conv_chunk, i) for i in range(tm // CONV_ROWS)]
    matmul_jobs = [functools.partial(project_gate, 0, mrg), functools.partial(project_gate, 1, g1_scr),
                   functools.partial(project_gate, 2, g2_scr)]
    matmul_jobs += [functools.partial(pool_group, gi) for gi in range(len(POOL_WINDOWS))]

    def run_next(jobs):
        if jobs:
            jobs.pop(0)()

    run_next(vector_jobs)
    project_q()
    run_next(vector_jobs)
    project_kv()
    n_blocks = tm // WINDOW
    raw = attn_qk(0)
    run_next(matmul_jobs)
    run_next(vector_jobs)
    for i in range(n_blocks):
        raw_next = attn_qk(i + 1) if i + 1 < n_blocks else None
        probs = attn_softmax(i, raw)
        run_next(matmul_jobs)
        attn_pv(i, probs)
        run_next(vector_jobs)
        raw = raw_next
    while vector_jobs or matmul_jobs:
        run_next(vector_jobs)
        run_next(matmul_jobs)

    yb = _dot(s_scr[...], wco_ref[...])
    mrg[...] = mrg[...] + g1_scr[...] * yb
    yc = _dot(o_scr[...], wao_ref[...])
    merged = mrg[...] + g2_scr[...] * yc
    ho_ref[...] = x + _dot(merged.astype(BF16), wo_ref[...])


def _prompt_mixer(layer, hp, t_valid, tm, sinks, norm_mix, w_in, b_in, pool_w, pool_scale,
                  conv_w, conv_b, conv_ng, conv_nb, w_conv_out, bias_tab, w_attn_out, w_out):
    nb, tp, _ = hp.shape
    n_tiles = tp // tm
    last_tile = (t_valid - 1) // tm
    e_loc = t_valid - last_tile * tm
    assert last_tile == n_tiles - 1 and e_loc >= WINDOW and e_loc % SUBLANES == 0
    body = functools.partial(_prompt_mixer_body, layer=layer, tm=tm, e_loc=e_loc)
    lsel3 = (layer, 0, 0)
    in_specs = [
        pl.BlockSpec(memory_space=pltpu.SMEM),
        pl.BlockSpec((None, tm, D_MODEL), lambda b, t: (b, t, 0)),
        _const_spec((None, 1, D_MODEL), lsel3),
        _const_spec((None, D_MODEL, D_IN), lsel3),
        _const_spec((None, 1, D_IN), lsel3),
        _const_spec((None, len(POOL_WINDOWS), POOL_GW, POOL_OUT_GW), (layer, 0, 0, 0)),
        _const_spec((None, 1, D_MODEL), lsel3),
        _const_spec((None, CONV_WIDTH, CONV_DIM), lsel3),
        _const_spec((None, 1, CONV_DIM), lsel3),
        _const_spec((None, 1, CONV_DIM), lsel3),
        _const_spec((None, 1, CONV_DIM), lsel3),
        _const_spec((None, CONV_DIM, D_MODEL), lsel3),
        _const_spec((2, N_KV_HEADS, 2 * WINDOW, GQA_GROUP * WINDOW), (0, 0, 0, 0)),
        _const_spec((None, D_MODEL, D_MODEL), lsel3),
        _const_spec((None, D_MODEL, D_MODEL), lsel3),
    ]
    out_shape = [
        jax.ShapeDtypeStruct((nb, tp, D_MODEL), F32),
        jax.ShapeDtypeStruct((nb, POOL_CARRY, POOL_WIDTH), F32),
        jax.ShapeDtypeStruct((nb, CONV_CARRY, CONV_DIM), F32),
        jax.ShapeDtypeStruct((nb, WINDOW, KV_DIM), F32),
        jax.ShapeDtypeStruct((nb, WINDOW, KV_DIM), F32),
    ]
    out_specs = [
        pl.BlockSpec((None, tm, D_MODEL), lambda b, t: (b, t, 0)),
        pl.BlockSpec((None, POOL_CARRY, POOL_WIDTH), lambda b, t: (b, 0, 0)),
        pl.BlockSpec((None, CONV_CARRY, CONV_DIM), lambda b, t: (b, 0, 0)),
        pl.BlockSpec((None, WINDOW, KV_DIM), lambda b, t: (b, 0, 0)),
        pl.BlockSpec((None, WINDOW, KV_DIM), lambda b, t: (b, 0, 0)),
    ]
    scratch = [
        pltpu.VMEM((tm, D_MODEL), BF16),
        pltpu.VMEM((POOL_CARRY + tm, POOL_WIDTH), F32),
        pltpu.VMEM((CONV_CARRY + tm, CONV_DIM), F32),
        pltpu.VMEM((WINDOW + tm, KV_DIM), BF16),
        pltpu.VMEM((KV_DIM, WINDOW + tm), BF16),
        pltpu.VMEM((tm, D_MODEL), BF16),
        pltpu.VMEM((tm, D_MODEL), BF16),
        pltpu.VMEM((tm, CONV_DIM), BF16),
        pltpu.VMEM((tm, D_MODEL), F32),
        pltpu.VMEM((2 * SUBLANES, CONV_ROWS + CONV_CARRY, LANES), F32),
        pltpu.VMEM((tm, D_MODEL), F32),
        pltpu.VMEM((tm, D_MODEL), F32),
    ]
    return pl.pallas_call(
        body,
        grid=(nb, n_tiles),
        in_specs=in_specs,
        out_specs=out_specs,
        out_shape=out_shape,
        scratch_shapes=scratch,
        compiler_params=pltpu.CompilerParams(
            dimension_semantics=("arbitrary", "arbitrary"), vmem_limit_bytes=VMEM_LIMIT),
        name=f"prompt_mixer_l{layer}",
    )(sinks, hp, norm_mix, w_in, b_in, pool_w, pool_scale, conv_w, conv_b, conv_ng, conv_nb,
      w_conv_out, bias_tab, w_attn_out, w_out)


def _mlp_rows(x, g_ref, wup_ref, wdn_ref):
    u = _rms(x, g_ref[...]).astype(BF16)
    acc = x
    for c0 in range(0, D_FF, FF_CHUNK):
        up = jnp.maximum(_dot(u, wup_ref[:, c0:c0 + FF_CHUNK]), 0.0)
        acc = acc + _dot((up * up).astype(BF16), wdn_ref[c0:c0 + FF_CHUNK, :])
    return acc


def _mlp_body(h_ref, g_ref, wup_ref, wdn_ref, gf_ref, o_ref, *, final_norm):
    y = _mlp_rows(h_ref[...], g_ref, wup_ref, wdn_ref)
    if final_norm:
        y = _rms(y, gf_ref[...])
    o_ref[...] = y


def _mlp(layer, h2d, tm, norm_mlp, w_up, w_down, norm_final, final_norm):
    rows = h2d.shape[0]
    lsel3 = (layer, 0, 0)
    return pl.pallas_call(
        functools.partial(_mlp_body, final_norm=final_norm),
        grid=(rows // tm,),
        in_specs=[
            pl.BlockSpec((tm, D_MODEL), lambda i: (i, 0)),
            _const_spec((None, 1, D_MODEL), lsel3),
            _const_spec((None, D_MODEL, D_FF), lsel3),
            _const_spec((None, D_FF, D_MODEL), lsel3),
            _const_spec((1, D_MODEL), (0, 0)),
        ],
        out_specs=pl.BlockSpec((tm, D_MODEL), lambda i: (i, 0)),
        out_shape=jax.ShapeDtypeStruct((rows, D_MODEL), F32),
        compiler_params=pltpu.CompilerParams(
            dimension_semantics=("arbitrary",), vmem_limit_bytes=VMEM_LIMIT),
        name=f"mlp_l{layer}",
    )(h2d, norm_mlp, w_up, w_down, norm_final)


def _sample_in_body(h_ref, nm_ref, win_ref, bin_ref, pw_ref, ps_ref, cw_ref, cb_ref, cng_ref,
                    cnb_ref, wco_ref, sp_ref, sc_ref, *rest, sb, tdec):
    mrg_ref, g2_ref, q_ref, kn_ref, vn_ref, po_ref, co_ref, exta, extc = rest[-9:]
    rows = sb * tdec
    x = h_ref[...]
    u = _rms(x, nm_ref[...]).astype(BF16)

    def proj(c0, width):
        return _dot(u, win_ref[:, c0:c0 + width]) + bin_ref[:, c0:c0 + width]

    def gate(i):
        return jax.nn.sigmoid(proj(C_GATE + i * D_MODEL, D_MODEL))

    a = proj(C_POOL, POOL_WIDTH)
    exta[:, 1:POOL_CARRY, :] = sp_ref[...]
    exta[:, POOL_CARRY:POOL_CARRY + tdec, :] = a.reshape(sb, tdec, POOL_WIDTH)
    ya_parts = []
    for gi, win in enumerate(POOL_WINDOWS):
        c0 = gi * POOL_GW
        own = a[:, c0:c0 + POOL_GW]
        s = own
        for j in range(1, win):
            s = s + exta[:, POOL_CARRY - j:POOL_CARRY - j + tdec, c0:c0 + POOL_GW].reshape(
                rows, POOL_GW)
        r = (s / float(win) - own).astype(BF16)
        ya_parts.append(_dot(r, pw_ref[gi]))
    ya = jnp.concatenate(ya_parts, axis=1) * ps_ref[...]
    po_ref[...] = exta[:, POOL_CARRY + tdec - POOL_HIST:POOL_CARRY + tdec, :]
    merged = gate(0) * ya

    c = proj(C_GLU, 2 * CONV_DIM)
    glu = c[:, :CONV_DIM] * jax.nn.sigmoid(c[:, CONV_DIM:])
    first = CONV_CARRY - CONV_HIST
    extc[:, first:CONV_CARRY, :] = sc_ref[...]
    extc[:, CONV_CARRY:CONV_CARRY + tdec, :] = glu.reshape(sb, tdec, CONV_DIM)
    acc = jnp.zeros((rows, CONV_DIM), F32)
    for j in range(CONV_WIDTH):
        acc = acc + extc[:, first + j:first + j + tdec, :].reshape(rows, CONV_DIM) * cw_ref[j:j + 1, :]
    y = acc + cb_ref[...]
    s_act = _layernorm_silu(y, cng_ref[...], cnb_ref[...]).astype(BF16)
    co_ref[...] = extc[:, CONV_CARRY + tdec - CONV_HIST:CONV_CARRY + tdec, :]
    merged = merged + gate(1) * _dot(s_act, wco_ref[...])

    mrg_ref[...] = merged
    g2_ref[...] = gate(2)
    q_ref[...] = proj(C_Q, N_HEADS * HEAD_DIM) * (HEAD_DIM ** -0.5)
    kv = proj(C_K, 2 * KV_DIM)
    kn_ref[...] = kv[:, :KV_DIM]
    vn_ref[...] = kv[:, KV_DIM:]


def _sample_in(layer, hs, sb, tdec, norm_mix, w_in, b_in, pool_w, pool_scale, conv_w, conv_b,
               conv_ng, conv_nb, w_conv_out, state_pool, state_conv, pool_all, conv_all):
    depth = state_pool.shape[0]
    rows_all = hs.shape[0]
    n_seq = rows_all // tdec
    rows = sb * tdec
    lsel3 = (layer, 0, 0)
    row_spec = lambda w: pl.BlockSpec((rows, w), lambda i: (i, 0))
    in_specs = [
        row_spec(D_MODEL),
        _const_spec((None, 1, D_MODEL), lsel3),
        _const_spec((None, D_MODEL, D_IN), lsel3),
        _const_spec((None, 1, D_IN), lsel3),
        _const_spec((None, len(POOL_WINDOWS), POOL_GW, POOL_OUT_GW), (layer, 0, 0, 0)),
        _const_spec((None, 1, D_MODEL), lsel3),
        _const_spec((None, CONV_WIDTH, CONV_DIM), lsel3),
        _const_spec((None, 1, CONV_DIM), lsel3),
        _const_spec((None, 1, CONV_DIM), lsel3),
        _const_spec((None, 1, CONV_DIM), lsel3),
        _const_spec((None, CONV_DIM, D_MODEL), lsel3),
        pl.BlockSpec((None, sb, POOL_HIST, POOL_WIDTH), lambda i: (layer, i, 0, 0)),
        pl.BlockSpec((None, sb, CONV_HIST, CONV_DIM), lambda i: (layer, i, 0, 0)),
    ]
    out_shape = [
        jax.ShapeDtypeStruct((rows_all, D_MODEL), F32),
        jax.ShapeDtypeStruct((rows_all, D_MODEL), F32),
        jax.ShapeDtypeStruct((rows_all, D_MODEL), F32),
        jax.ShapeDtypeStruct((rows_all, KV_DIM), F32),
        jax.ShapeDtypeStruct((rows_all, KV_DIM), F32),
        jax.ShapeDtypeStruct((depth, n_seq, POOL_HIST, POOL_WIDTH), F32),
        jax.ShapeDtypeStruct((depth, n_seq, CONV_HIST, CONV_DIM), F32),
    ]
    out_specs = [
        row_spec(D_MODEL), row_spec(D_MODEL), row_spec(D_MODEL), row_spec(KV_DIM), row_spec(KV_DIM),
        pl.BlockSpec((None, sb, POOL_HIST, POOL_WIDTH), lambda i: (layer, i, 0, 0)),
        pl.BlockSpec((None, sb, CONV_HIST, CONV_DIM), lambda i: (layer, i, 0, 0)),
    ]
    scratch = [
        pltpu.VMEM((sb, POOL_CARRY + tdec, POOL_WIDTH), F32),
        pltpu.VMEM((sb, CONV_CARRY + tdec, CONV_DIM), F32),
    ]
    args = [hs, norm_mix, w_in, b_in, pool_w, pool_scale, conv_w, conv_b, conv_ng, conv_nb,
            w_conv_out, state_pool, state_conv]
    extra_specs, extra_args, aliases = _alias_tail((pool_all, conv_all), len(args), 5)
    return pl.pallas_call(
        functools.partial(_sample_in_body, sb=sb, tdec=tdec),
        grid=(n_seq // sb,),
        in_specs=in_specs + extra_specs,
        out_specs=out_specs,
        out_shape=out_shape,
        input_output_aliases=aliases,
        scratch_shapes=scratch,
        compiler_params=pltpu.CompilerParams(
            dimension_semantics=("arbitrary",), vmem_limit_bytes=VMEM_LIMIT),
        name=f"sample_in_l{layer}",
    )(*(args + extra_args))


def _sample_attn_body(q_ref, kn_ref, vn_ref, ck_ref, cv_ref, bias_ref, sink_ref, *rest,
                      sb, tdec):
    o_ref, ko_ref, vo_ref, qbd, kx, vx = rest[-6:]
    n_keys = WINDOW + tdec

    @pl.when(pl.program_id(0) == 0)
    def _():
        qbd[...] = jnp.zeros(qbd.shape, F32)
        kx[...] = jnp.zeros(kx.shape, F32)
        vx[...] = jnp.zeros(vx.shape, F32)

    def stage_qk(s):
        r0 = s * tdec
        qs = q_ref[r0:r0 + tdec, :]
        kn = kn_ref[r0:r0 + tdec, :]
        vn = vn_ref[r0:r0 + tdec, :]
        kc = ck_ref[s]
        vc = cv_ref[s]
        for h in range(N_HEADS):
            g_kv = h // GQA_GROUP
            qbd[s, h * tdec:(h + 1) * tdec, g_kv * HEAD_DIM:(g_kv + 1) * HEAD_DIM] = (
                qs[:, h * HEAD_DIM:(h + 1) * HEAD_DIM])
        kx[s, 0:WINDOW, :] = kc
        kx[s, WINDOW:n_keys, :] = kn
        vx[s, 0:WINDOW, :] = vc
        vx[s, WINDOW:n_keys, :] = vn
        ko_ref[s, 0:WINDOW - tdec, :] = kc[tdec:, :]
        ko_ref[s, WINDOW - tdec:WINDOW, :] = kn
        vo_ref[s, 0:WINDOW - tdec, :] = vc[tdec:, :]
        vo_ref[s, WINDOW - tdec:WINDOW, :] = vn
        return _dot_nt(qbd[s].astype(BF16), kx[s].astype(BF16))

    def stage_pv(s, p):
        r0 = s * tdec
        o_full = _dot(p, vx[s].astype(BF16))
        for h in range(N_HEADS):
            g_kv = h // GQA_GROUP
            o_ref[r0:r0 + tdec, h * HEAD_DIM:(h + 1) * HEAD_DIM] = (
                o_full[h * tdec:(h + 1) * tdec, g_kv * HEAD_DIM:(g_kv + 1) * HEAD_DIM])

    raw = [stage_qk(s) for s in range(sb)]
    probs = [_softmax_with_sink(r + bias_ref[...], sink_ref[...]).astype(BF16) for r in raw]
    for s in range(sb):
        stage_pv(s, probs[s])


def _alias_tail(stacked, n_args, first_out):
    if stacked[0] is None:
        return [], [], {}
    specs = [pl.BlockSpec(memory_space=pl.ANY) for _ in stacked]
    return specs, list(stacked), {n_args + i: first_out + i for i in range(len(stacked))}


def _sample_attn(layer, q, kn, vn, cache_k, cache_v, bias_s, sink_col, k_all, v_all, sb, tdec):
    depth = cache_k.shape[0]
    rows_all = q.shape[0]
    n_seq = rows_all // tdec
    rows = sb * tdec
    key_pad = 2 * WINDOW
    row_spec = lambda w: pl.BlockSpec((rows, w), lambda i: (i, 0))
    cache_spec = pl.BlockSpec((None, sb, WINDOW, KV_DIM), lambda i: (layer, i, 0, 0))
    in_specs = [
        row_spec(D_MODEL), row_spec(KV_DIM), row_spec(KV_DIM), cache_spec, cache_spec,
        _const_spec((N_HEADS * tdec, key_pad), (0, 0)),
        _const_spec((None, N_HEADS * tdec, 1), (layer, 0, 0)),
    ]
    args = [q, kn, vn, cache_k, cache_v, bias_s, sink_col]
    extra_specs, extra_args, aliases = _alias_tail((k_all, v_all), len(args), 1)
    stacked = jax.ShapeDtypeStruct((depth, n_seq, WINDOW, KV_DIM), F32)
    return pl.pallas_call(
        functools.partial(_sample_attn_body, sb=sb, tdec=tdec),
        grid=(n_seq // sb,),
        in_specs=in_specs + extra_specs,
        out_specs=[row_spec(D_MODEL), cache_spec, cache_spec],
        out_shape=[jax.ShapeDtypeStruct((rows_all, D_MODEL), F32), stacked, stacked],
        input_output_aliases=aliases,
        scratch_shapes=[
            pltpu.VMEM((sb, N_HEADS * tdec, KV_DIM), F32),
            pltpu.VMEM((sb, key_pad, KV_DIM), F32),
            pltpu.VMEM((sb, key_pad, KV_DIM), F32),
        ],
        compiler_params=pltpu.CompilerParams(
            dimension_semantics=("arbitrary",), vmem_limit_bytes=VMEM_LIMIT),
        name=f"sample_attn_l{layer}",
    )(*(args + extra_args))


def _sample_out_body(h_ref, mrg_ref, g2_ref, o_ref, wao_ref, wo_ref, g_ref, wup_ref, wdn_ref,
                     gf_ref, out_ref, *, final_norm):
    yc = _dot(o_ref[...].astype(BF16), wao_ref[...])
    merged = mrg_ref[...] + g2_ref[...] * yc
    h1 = h_ref[...] + _dot(merged.astype(BF16), wo_ref[...])
    y = _mlp_rows(h1, g_ref, wup_ref, wdn_ref)
    if final_norm:
        y = _rms(y, gf_ref[...])
    out_ref[...] = y


def _sample_out(layer, hs, mrg, g2, o, tm, w_attn_out, w_out, norm_mlp, w_up, w_down,
                norm_final, final_norm):
    rows = hs.shape[0]
    lsel3 = (layer, 0, 0)
    row_spec = pl.BlockSpec((tm, D_MODEL), lambda i: (i, 0))
    return pl.pallas_call(
        functools.partial(_sample_out_body, final_norm=final_norm),
        grid=(rows // tm,),
        in_specs=[
            row_spec, row_spec, row_spec, row_spec,
            _const_spec((None, D_MODEL, D_MODEL), lsel3),
            _const_spec((None, D_MODEL, D_MODEL), lsel3),
            _const_spec((None, 1, D_MODEL), lsel3),
            _const_spec((None, D_MODEL, D_FF), lsel3),
            _const_spec((None, D_FF, D_MODEL), lsel3),
            _const_spec((1, D_MODEL), (0, 0)),
        ],
        out_specs=row_spec,
        out_shape=jax.ShapeDtypeStruct((rows, D_MODEL), F32),
        compiler_params=pltpu.CompilerParams(
            dimension_semantics=("arbitrary",), vmem_limit_bytes=VMEM_LIMIT),
        name=f"sample_out_l{layer}",
    )(hs, mrg, g2, o, w_attn_out, w_out, norm_mlp, w_up, w_down, norm_final)


def _forward(x_prompt, x_sample, state_pool, state_conv, cache_k, cache_v, meta_tokens,
             rel_bias, norm_mix, w_in, b_in, pool_w, pool_scale, conv_w, conv_b, conv_norm_g,
             conv_norm_b, w_conv_out, attn_sinks, w_attn_out, w_out, norm_mlp, w_up, w_down,
             norm_final, *, past_len, tm_prompt, tm_sample, sb_in, sb_attn):
    depth = w_in.shape[0]
    nb, seq, _ = x_prompt.shape
    n_seq, tdec, _ = x_sample.shape
    assert past_len >= POOL_HIST and past_len >= WINDOW
    t_valid = seq + N_META
    tp = -(-t_valid // tm_prompt) * tm_prompt

    meta = jnp.broadcast_to(meta_tokens[None].astype(F32), (nb, N_META, D_MODEL))
    hp = jnp.concatenate(
        [meta, x_prompt, jnp.zeros((nb, tp - t_valid, D_MODEL), F32)], axis=1)
    hs = x_sample.reshape(n_seq * tdec, D_MODEL)

    row3 = lambda a: a.reshape(depth, 1, a.shape[-1])
    w_in_b, pool_w_b, wco_b = w_in.astype(BF16), pool_w.astype(BF16), w_conv_out.astype(BF16)
    wao_b, wo_b = w_attn_out.astype(BF16), w_out.astype(BF16)
    w_up_b, w_down_b = w_up.astype(BF16), w_down.astype(BF16)
    nm3, bin3, ps3 = row3(norm_mix), row3(b_in), row3(pool_scale)
    cb3, cng3, cnb3, nmlp3 = row3(conv_b), row3(conv_norm_g), row3(conv_norm_b), row3(norm_mlp)
    nf2 = norm_final.reshape(1, D_MODEL)

    band = _banded_bias(rel_bias)
    own_only = jnp.asarray(np.arange(2 * WINDOW) >= WINDOW)
    bias_tab = jnp.stack([band, jnp.where(own_only, band, -jnp.inf)])
    bias_tab = bias_tab.reshape(2, N_KV_HEADS, GQA_GROUP, WINDOW, 2 * WINDOW)
    bias_tab = bias_tab.transpose(0, 1, 4, 2, 3).reshape(
        2, N_KV_HEADS, 2 * WINDOW, GQA_GROUP * WINDOW)
    bias_s = band[:, :tdec, :].reshape(N_HEADS * tdec, 2 * WINDOW)
    sink_col = jnp.repeat(attn_sinks.astype(F32), tdec, axis=1).reshape(depth, N_HEADS * tdec, 1)

    cache_k2 = cache_k.reshape(depth, n_seq, WINDOW, KV_DIM)
    cache_v2 = cache_v.reshape(depth, n_seq, WINDOW, KV_DIM)
    sinks = attn_sinks.astype(F32)

    pool_p, conv_p, k_p, v_p = [], [], [], []
    pool_s = conv_s = k_s = v_s = None
    for l in range(depth):
        last = l == depth - 1
        hp, pst, cst, kst, vst = _prompt_mixer(
            l, hp, t_valid, tm_prompt, sinks, nm3, w_in_b, bin3, pool_w_b, ps3, conv_w, cb3,
            cng3, cnb3, wco_b, bias_tab, wao_b, wo_b)
        hp = _mlp(l, hp.reshape(nb * tp, D_MODEL), tm_prompt, nmlp3, w_up_b, w_down_b, nf2,
                  last).reshape(nb, tp, D_MODEL)
        pool_p.append(pst[:, POOL_CARRY - POOL_HIST:])
        conv_p.append(cst[:, CONV_CARRY - CONV_HIST:])
        k_p.append(kst.reshape(nb, WINDOW, N_KV_HEADS, HEAD_DIM))
        v_p.append(vst.reshape(nb, WINDOW, N_KV_HEADS, HEAD_DIM))

        mrg, g2, q, kn, vn, pool_s, conv_s = _sample_in(
            l, hs, sb_in, tdec, nm3, w_in_b, bin3, pool_w_b, ps3, conv_w, cb3, cng3, cnb3,
            wco_b, state_pool, state_conv, pool_s, conv_s)
        o, k_s, v_s = _sample_attn(l, q, kn, vn, cache_k2, cache_v2, bias_s, sink_col, k_s, v_s,
                                   sb_attn, tdec)
        hs = _sample_out(l, hs, mrg, g2, o, tm_sample, wao_b, wo_b, nmlp3, w_up_b, w_down_b,
                         nf2, last)

    y_prompt = hp[:, N_META:t_valid]
    y_sample = hs.reshape(n_seq, tdec, D_MODEL)
    kv_shape = (depth, n_seq, WINDOW, N_KV_HEADS, HEAD_DIM)
    return (y_prompt, y_sample, jnp.stack(pool_p), pool_s, jnp.stack(conv_p), conv_s,
            jnp.stack(k_p), k_s.reshape(kv_shape), jnp.stack(v_p), v_s.reshape(kv_shape))


def kernel(x_prompt, x_sample, state_pool, state_conv, cache_k, cache_v, meta_tokens, rel_bias, norm_mix, w_in, b_in, pool_w, pool_scale, conv_w, conv_b, conv_norm_g, conv_norm_b, w_conv_out, attn_sinks, w_attn_out, w_out, norm_mlp, w_up, w_down, norm_final):
    return _forward(
        x_prompt, x_sample, state_pool, state_conv, cache_k, cache_v, meta_tokens, rel_bias,
        norm_mix, w_in, b_in, pool_w, pool_scale, conv_w, conv_b, conv_norm_g, conv_norm_b,
        w_conv_out, attn_sinks, w_attn_out, w_out, norm_mlp, w_up, w_down, norm_final,
        past_len=8192, tm_prompt=640, tm_sample=512, sb_in=32, sb_attn=8)
```

```python
import functools

import numpy as np
import jax
import jax.numpy as jnp
from jax import lax
from jax.experimental import pallas as pl
from jax.experimental.pallas import tpu as pltpu

F32 = jnp.float32
BF16 = jnp.bfloat16

D_MODEL = 1024
N_META = 16
POOL_WINDOWS = (2, 4, 8, 16)
POOL_WIDTH = 512
POOL_GW = 128
POOL_OUT_GW = 256
POOL_HIST = 15
CONV_DIM = 512
CONV_WIDTH = 31
CONV_HIST = 30
HEAD_DIM = 64
N_HEADS = 16
N_KV_HEADS = 4
GQA_GROUP = 4
KV_DIM = N_KV_HEADS * HEAD_DIM
WINDOW = 128
N_BUCKETS = 32
MAX_DISTANCE = 128
D_FF = 4096
NORM_EPS = 1e-6
C_POOL = 0
C_GLU = C_POOL + POOL_WIDTH
C_Q = C_GLU + 2 * CONV_DIM
C_K = C_Q + N_HEADS * HEAD_DIM
C_V = C_K + KV_DIM
C_GATE = C_V + KV_DIM
D_IN = C_GATE + 3 * D_MODEL

SUBLANES = 8
LANES = 128
POOL_CARRY = 16
CONV_CARRY = 32
FF_CHUNK = 1024
CONV_ROWS = 128
NORM_ROWS = 32
VMEM_LIMIT = 60 * 1024 * 1024


def _dot(a, b):
    return jnp.dot(a, b, preferred_element_type=F32)


def _dot_nt(a, b):
    return lax.dot_general(a, b, (((1,), (1,)), ((), ())), preferred_element_type=F32)


def _rms(x, g):
    ms = jnp.mean(x * x, axis=-1, keepdims=True)
    return x * lax.rsqrt(ms + NORM_EPS) * g


def _layernorm_silu(y, g, b):
    mu = jnp.mean(y, axis=-1, keepdims=True)
    yc = y - mu
    var = jnp.mean(yc * yc, axis=-1, keepdims=True)
    yn = yc * lax.rsqrt(var + NORM_EPS) * g + b
    return yn * jax.nn.sigmoid(yn)


def _softmax_with_sink(logits, sink):
    m = jnp.maximum(jnp.max(logits, axis=-1, keepdims=True), sink)
    p = jnp.exp(logits - m)
    denom = jnp.sum(p, axis=-1, keepdims=True) + jnp.exp(sink - m)
    return p * (1.0 / denom)


def _const_spec(block_shape, index):
    return pl.BlockSpec(block_shape, lambda *_: index, pipeline_mode=pl.Buffered(1))


def _t5_bucket(dist):
    n = np.maximum(dist, 0)
    exact = N_BUCKETS // 2
    large = exact + (np.log(np.maximum(n, 1) / exact) / np.log(MAX_DISTANCE / exact)
                     * (N_BUCKETS - exact)).astype(np.int32)
    large = np.minimum(large, N_BUCKETS - 1)
    return np.where(n < exact, n, large).astype(np.int32)


def _banded_bias(rel_bias):
    n_diag = 3 * WINDOW - 1
    per_dist = rel_bias[_t5_bucket(np.arange(WINDOW + 1))].astype(F32).T
    profile = jnp.full((N_HEADS, n_diag + 1), -jnp.inf, F32)
    profile = profile.at[:, WINDOW - 1:2 * WINDOW].set(per_dist[:, ::-1])
    skew = jnp.tile(profile, (1, WINDOW))[:, :WINDOW * n_diag].reshape(N_HEADS, WINDOW, n_diag)
    return skew[:, :, WINDOW - 1:WINDOW - 1 + 2 * WINDOW]


def _prompt_mixer_body(sinks_ref, h_ref, nm_ref, win_ref, bin_ref, pw_ref, ps_ref, cw_ref,
                       cb_ref, cng_ref, cnb_ref, wco_ref, bias_ref, wao_ref, wo_ref,
                       ho_ref, pst_ref, cst_ref, kst_ref, vst_ref,
                       u_scr, exta, extc, kbuf, vbuf, q_scr, o_scr, s_scr, mrg, xsh, g1_scr, g2_scr,
                       y_scr, *, layer, tm, e_loc):
    t = pl.program_id(1)

    @pl.when(t == 0)
    def _():
        exta[0:POOL_CARRY, :] = jnp.zeros((POOL_CARRY, POOL_WIDTH), F32)
        extc[0:CONV_CARRY, :] = jnp.zeros((CONV_CARRY, CONV_DIM), F32)
        kbuf[0:WINDOW, :] = jnp.zeros((WINDOW, KV_DIM), BF16)
        vbuf[:, 0:WINDOW] = jnp.zeros((KV_DIM, WINDOW), BF16)

    @pl.when(t > 0)
    def _():
        exta[0:POOL_CARRY, :] = exta[tm:tm + POOL_CARRY, :]
        extc[0:CONV_CARRY, :] = extc[tm:tm + CONV_CARRY, :]
        kbuf[0:WINDOW, :] = kbuf[tm:tm + WINDOW, :]
        vbuf[:, 0:WINDOW] = vbuf[:, tm:tm + WINDOW]

    u_scr[...] = _rms(h_ref[...], nm_ref[...]).astype(BF16)

    def proj(c0, width):
        return _dot(u_scr[...], win_ref[:, c0:c0 + width]) + bin_ref[:, c0:c0 + width]

    def gate(i):
        return jax.nn.sigmoid(proj(C_GATE + i * D_MODEL, D_MODEL))

    c = proj(C_GLU, 2 * CONV_DIM)
    extc[CONV_CARRY:CONV_CARRY + tm, :] = c[:, :CONV_DIM] * jax.nn.sigmoid(c[:, CONV_DIM:])
    a = proj(C_POOL, POOL_WIDTH)
    exta[POOL_CARRY:POOL_CARRY + tm, :] = a

    def project_q():
        q_scr[...] = (proj(C_Q, N_HEADS * HEAD_DIM) * (HEAD_DIM ** -0.5)).astype(BF16)

    def project_kv():
        kv = proj(C_K, 2 * KV_DIM)
        kbuf[WINDOW:WINDOW + tm, :] = kv[:, :KV_DIM].astype(BF16)
        vbuf[:, WINDOW:WINDOW + tm] = kv[:, KV_DIM:].T.astype(BF16)
        kst_ref[...] = kv[e_loc - WINDOW:e_loc, :KV_DIM]
        vst_ref[...] = kv[e_loc - WINDOW:e_loc, KV_DIM:]

    def project_gate(i, dst):
        dst[...] = gate(i)

    pst_ref[...] = exta[e_loc:e_loc + POOL_CARRY, :]
    cst_ref[...] = extc[e_loc:e_loc + CONV_CARRY, :]

    first = CONV_CARRY - CONV_HIST
    last_off = first + CONV_WIDTH - 1

    def conv_chunk(ci):
        r0 = ci * CONV_ROWS
        for li in range(CONV_DIM // LANES):
            c0 = li * LANES
            slot = ((ci * (CONV_DIM // LANES) + li) % 2) * SUBLANES
            acc = jnp.zeros((CONV_ROWS, LANES), F32)
            for sh in range(SUBLANES):
                taps = [a0 for a0 in range(0, last_off + 1, SUBLANES) if first <= a0 + sh <= last_off]
                n = taps[-1] + CONV_ROWS
                if sh:
                    xsh[slot + sh, 0:n, :] = extc[r0 + sh:r0 + sh + n, c0:c0 + LANES]
                for a0 in taps:
                    w_row = cw_ref[a0 + sh - first:a0 + sh - first + 1, c0:c0 + LANES]
                    if sh:
                        acc = acc + xsh[slot + sh, a0:a0 + CONV_ROWS, :] * w_row
                    else:
                        acc = acc + extc[r0 + a0:r0 + a0 + CONV_ROWS, c0:c0 + LANES] * w_row
            y_scr[r0:r0 + CONV_ROWS, c0:c0 + LANES] = acc + cb_ref[:, c0:c0 + LANES]
        for r1 in range(r0, r0 + CONV_ROWS, NORM_ROWS):
            s_scr[r1:r1 + NORM_ROWS, :] = _layernorm_silu(
                y_scr[r1:r1 + NORM_ROWS, :], cng_ref[...], cnb_ref[...]).astype(BF16)

    def attn_qk(qb):
        r0 = qb * WINDOW
        raw = []
        for g_kv in range(N_KV_HEADS):
            heads = range(g_kv * GQA_GROUP, (g_kv + 1) * GQA_GROUP)
            q4 = jnp.concatenate(
                [q_scr[r0:r0 + WINDOW, h * HEAD_DIM:(h + 1) * HEAD_DIM] for h in heads], axis=0)
            kw = kbuf[r0:r0 + 2 * WINDOW, g_kv * HEAD_DIM:(g_kv + 1) * HEAD_DIM]
            raw.append(_dot_nt(kw, q4))
        return raw

    def attn_softmax(qb, raw):
        sel = jnp.where(t == 0, 1, 0) if qb == 0 else 0
        probs = []
        for g_kv in range(N_KV_HEADS):
            heads = range(g_kv * GQA_GROUP, (g_kv + 1) * GQA_GROUP)
            logits = raw[g_kv] + bias_ref[sel, g_kv]
            sink = jnp.concatenate(
                [jnp.full((1, WINDOW), sinks_ref[layer, h], F32) for h in heads], axis=1)
            m = jnp.maximum(jnp.max(logits, axis=0, keepdims=True), sink)
            p = jnp.exp(logits - m)
            denom = jnp.sum(p, axis=0, keepdims=True) + jnp.exp(sink - m)
            probs.append((p.astype(BF16), 1.0 / denom))
        return probs

    def attn_pv(qb, probs):
        r0 = qb * WINDOW
        outs = []
        for g_kv in range(N_KV_HEADS):
            p, inv_denom = probs[g_kv]
            vw_t = vbuf[g_kv * HEAD_DIM:(g_kv + 1) * HEAD_DIM, r0:r0 + 2 * WINDOW]
            o_t = _dot(vw_t, p) * inv_denom
            outs.extend(o_t[:, g * WINDOW:(g + 1) * WINDOW] for g in range(GQA_GROUP))
        o_scr[r0:r0 + WINDOW, :] = jnp.concatenate(outs, axis=0).T.astype(BF16)

    def pool_group(gi):
        win = POOL_WINDOWS[gi]
        pos = t * tm + lax.broadcasted_iota(jnp.int32, (tm, 1), 0)
        c0 = gi * POOL_GW
        own = exta[POOL_CARRY:POOL_CARRY + tm, c0:c0 + POOL_GW]
        s = own
        for j in range(1, win):
            s = s + exta[POOL_CARRY - j:POOL_CARRY - j + tm, c0:c0 + POOL_GW]
        cnt = jnp.minimum(pos + 1, win).astype(F32)
        r = (s / cnt - own).astype(BF16)
        o0 = gi * POOL_OUT_GW
        ya = _dot(r, pw_ref[gi]) * ps_ref[:, o0:o0 + POOL_OUT_GW]
        mrg[:, o0:o0 + POOL_OUT_GW] = mrg[:, o0:o0 + POOL_OUT_GW] * ya

    vector_jobs = [functools.partial(conv_chunk, i) for i in range(tm // CONV_ROWS)]
    matmul_jobs = [functools.partial(project_gate, 0, mrg), functools.partial(project_gate, 1, g1_scr),
                   functools.partial(project_gate, 2, g2_scr)]
    matmul_jobs += [functools.partial(pool_group, gi) for gi in range(len(POOL_WINDOWS))]

    def run_next(jobs):
        if jobs:
            jobs.pop(0)()

    run_next(vector_jobs)
    project_q()
    run_next(vector_jobs)
    project_kv()
    n_blocks = tm // WINDOW
    raw = attn_qk(0)
    run_next(matmul_jobs)
    run_next(vector_jobs)
    for i in range(n_blocks):
        raw_next = attn_qk(i + 1) if i + 1 < n_blocks else None
        probs = attn_softmax(i, raw)
        run_next(matmul_jobs)
        attn_pv(i, probs)
        run_next(vector_jobs)
        raw = raw_next
    while vector_jobs or matmul_jobs:
        run_next(vector_jobs)
        run_next(matmul_jobs)

    yb = _dot(s_scr[...], wco_ref[...])
    mrg[...] = mrg[...] + g1_scr[...] * yb
    yc = _dot(o_scr[...], wao_ref[...])
    merged = mrg[...] + g2_scr[...] * yc
    ho_ref[...] = h_ref[...] + _dot(merged.astype(BF16), wo_ref[...])


def _prompt_mixer(layer, hp, t_valid, tm, sinks, norm_mix, w_in, b_in, pool_w, pool_scale,
                  conv_w, conv_b, conv_ng, conv_nb, w_conv_out, bias_tab, w_attn_out, w_out):
    nb, tp, _ = hp.shape
    n_tiles = tp // tm
    last_tile = (t_valid - 1) // tm
    e_loc = t_valid - last_tile * tm
    assert last_tile == n_tiles - 1 and e_loc >= WINDOW and e_loc % SUBLANES == 0
    body = functools.partial(_prompt_mixer_body, layer=layer, tm=tm, e_loc=e_loc)
    lsel3 = (layer, 0, 0)
    in_specs = [
        pl.BlockSpec(memory_space=pltpu.SMEM),
        pl.BlockSpec((None, tm, D_MODEL), lambda b, t: (b, t, 0)),
        _const_spec((None, 1, D_MODEL), lsel3),
        _const_spec((None, D_MODEL, D_IN), lsel3),
        _const_spec((None, 1, D_IN), lsel3),
        _const_spec((None, len(POOL_WINDOWS), POOL_GW, POOL_OUT_GW), (layer, 0, 0, 0)),
        _const_spec((None, 1, D_MODEL), lsel3),
        _const_spec((None, CONV_WIDTH, CONV_DIM), lsel3),
        _const_spec((None, 1, CONV_DIM), lsel3),
        _const_spec((None, 1, CONV_DIM), lsel3),
        _const_spec((None, 1, CONV_DIM), lsel3),
        _const_spec((None, CONV_DIM, D_MODEL), lsel3),
        _const_spec((2, N_KV_HEADS, 2 * WINDOW, GQA_GROUP * WINDOW), (0, 0, 0, 0)),
        _const_spec((None, D_MODEL, D_MODEL), lsel3),
        _const_spec((None, D_MODEL, D_MODEL), lsel3),
    ]
    out_shape = [
        jax.ShapeDtypeStruct((nb, tp, D_MODEL), F32),
        jax.ShapeDtypeStruct((nb, POOL_CARRY, POOL_WIDTH), F32),
        jax.ShapeDtypeStruct((nb, CONV_CARRY, CONV_DIM), F32),
        jax.ShapeDtypeStruct((nb, WINDOW, KV_DIM), F32),
        jax.ShapeDtypeStruct((nb, WINDOW, KV_DIM), F32),
    ]
    out_specs = [
        pl.BlockSpec((None, tm, D_MODEL), lambda b, t: (b, t, 0)),
        pl.BlockSpec((None, POOL_CARRY, POOL_WIDTH), lambda b, t: (b, 0, 0)),
        pl.BlockSpec((None, CONV_CARRY, CONV_DIM), lambda b, t: (b, 0, 0)),
        pl.BlockSpec((None, WINDOW, KV_DIM), lambda b, t: (b, 0, 0)),
        pl.BlockSpec((None, WINDOW, KV_DIM), lambda b, t: (b, 0, 0)),
    ]
    scratch = [
        pltpu.VMEM((tm, D_MODEL), BF16),
        pltpu.VMEM((POOL_CARRY + tm, POOL_WIDTH), F32),
        pltpu.VMEM((CONV_CARRY + tm, CONV_DIM), F32),
        pltpu.VMEM((WINDOW + tm, KV_DIM), BF16),
        pltpu.VMEM((KV_DIM, WINDOW + tm), BF16),
        pltpu.VMEM((tm, D_MODEL), BF16),
        pltpu.VMEM((tm, D_MODEL), BF16),
        pltpu.VMEM((tm, CONV_DIM), BF16),
        pltpu.VMEM((tm, D_MODEL), F32),
        pltpu.VMEM((2 * SUBLANES, CONV_ROWS + CONV_CARRY, LANES), F32),
        pltpu.VMEM((tm, D_MODEL), F32),
        pltpu.VMEM((tm, D_MODEL), F32),
        pltpu.VMEM((tm, CONV_DIM), F32),
    ]
    return pl.pallas_call(
        body,
        grid=(nb, n_tiles),
        in_specs=in_specs,
        out_specs=out_specs,
        out_shape=out_shape,
        scratch_shapes=scratch,
        compiler_params=pltpu.CompilerParams(
            dimension_semantics=("arbitrary", "arbitrary"), vmem_limit_bytes=VMEM_LIMIT),
        name=f"prompt_mixer_l{layer}",
    )(sinks, hp, norm_mix, w_in, b_in, pool_w, pool_scale, conv_w, conv_b, conv_ng, conv_nb,
      w_conv_out, bias_tab, w_attn_out, w_out)


def _mlp_rows(x, g_ref, wup_ref, wdn_ref):
    u = _rms(x, g_ref[...]).astype(BF16)
    acc = x
    for c0 in range(0, D_FF, FF_CHUNK):
        up = jnp.maximum(_dot(u, wup_ref[:, c0:c0 + FF_CHUNK]), 0.0)
        acc = acc + _dot((up * up).astype(BF16), wdn_ref[c0:c0 + FF_CHUNK, :])
    return acc


def _mlp_body(h_ref, g_ref, wup_ref, wdn_ref, gf_ref, o_ref, *, final_norm):
    y = _mlp_rows(h_ref[...], g_ref, wup_ref, wdn_ref)
    if final_norm:
        y = _rms(y, gf_ref[...])
    o_ref[...] = y


def _mlp(layer, h2d, tm, norm_mlp, w_up, w_down, norm_final, final_norm):
    rows = h2d.shape[0]
    lsel3 = (layer, 0, 0)
    return pl.pallas_call(
        functools.partial(_mlp_body, final_norm=final_norm),
        grid=(rows // tm,),
        in_specs=[
            pl.BlockSpec((tm, D_MODEL), lambda i: (i, 0)),
            _const_spec((None, 1, D_MODEL), lsel3),
            _const_spec((None, D_MODEL, D_FF), lsel3),
            _const_spec((None, D_FF, D_MODEL), lsel3),
            _const_spec((1, D_MODEL), (0, 0)),
        ],
        out_specs=pl.BlockSpec((tm, D_MODEL), lambda i: (i, 0)),
        out_shape=jax.ShapeDtypeStruct((rows, D_MODEL), F32),
        compiler_params=pltpu.CompilerParams(
            dimension_semantics=("arbitrary",), vmem_limit_bytes=VMEM_LIMIT),
        name=f"mlp_l{layer}",
    )(h2d, norm_mlp, w_up, w_down, norm_final)


def _sample_in_body(h_ref, nm_ref, win_ref, bin_ref, pw_ref, ps_ref, cw_ref, cb_ref, cng_ref,
                    cnb_ref, wco_ref, sp_ref, sc_ref,
                    mrg_ref, g2_ref, q_ref, kn_ref, vn_ref, po_ref, co_ref,
                    exta, extc, *, sb, tdec):
    rows = sb * tdec
    x = h_ref[...]
    u = _rms(x, nm_ref[...]).astype(BF16)

    def proj(c0, width):
        return _dot(u, win_ref[:, c0:c0 + width]) + bin_ref[:, c0:c0 + width]

    def gate(i):
        return jax.nn.sigmoid(proj(C_GATE + i * D_MODEL, D_MODEL))

    a = proj(C_POOL, POOL_WIDTH)
    exta[:, 1:POOL_CARRY, :] = sp_ref[...]
    exta[:, POOL_CARRY:POOL_CARRY + tdec, :] = a.reshape(sb, tdec, POOL_WIDTH)
    ya_parts = []
    for gi, win in enumerate(POOL_WINDOWS):
        c0 = gi * POOL_GW
        own = a[:, c0:c0 + POOL_GW]
        s = own
        for j in range(1, win):
            s = s + exta[:, POOL_CARRY - j:POOL_CARRY - j + tdec, c0:c0 + POOL_GW].reshape(
                rows, POOL_GW)
        r = (s / float(win) - own).astype(BF16)
        ya_parts.append(_dot(r, pw_ref[gi]))
    ya = jnp.concatenate(ya_parts, axis=1) * ps_ref[...]
    po_ref[...] = exta[:, POOL_CARRY + tdec - POOL_HIST:POOL_CARRY + tdec, :]
    merged = gate(0) * ya

    c = proj(C_GLU, 2 * CONV_DIM)
    glu = c[:, :CONV_DIM] * jax.nn.sigmoid(c[:, CONV_DIM:])
    first = CONV_CARRY - CONV_HIST
    extc[:, first:CONV_CARRY, :] = sc_ref[...]
    extc[:, CONV_CARRY:CONV_CARRY + tdec, :] = glu.reshape(sb, tdec, CONV_DIM)
    acc = jnp.zeros((rows, CONV_DIM), F32)
    for j in range(CONV_WIDTH):
        acc = acc + extc[:, first + j:first + j + tdec, :].reshape(rows, CONV_DIM) * cw_ref[j:j + 1, :]
    y = acc + cb_ref[...]
    s_act = _layernorm_silu(y, cng_ref[...], cnb_ref[...]).astype(BF16)
    co_ref[...] = extc[:, CONV_CARRY + tdec - CONV_HIST:CONV_CARRY + tdec, :]
    merged = merged + gate(1) * _dot(s_act, wco_ref[...])

    mrg_ref[...] = merged
    g2_ref[...] = gate(2)
    q_ref[...] = proj(C_Q, N_HEADS * HEAD_DIM) * (HEAD_DIM ** -0.5)
    kv = proj(C_K, 2 * KV_DIM)
    kn_ref[...] = kv[:, :KV_DIM]
    vn_ref[...] = kv[:, KV_DIM:]


def _sample_in(layer, hs, sb, tdec, norm_mix, w_in, b_in, pool_w, pool_scale, conv_w, conv_b,
               conv_ng, conv_nb, w_conv_out, state_pool, state_conv):
    depth = state_pool.shape[0]
    rows_all = hs.shape[0]
    n_seq = rows_all // tdec
    rows = sb * tdec
    lsel3 = (layer, 0, 0)
    row_spec = lambda w: pl.BlockSpec((rows, w), lambda i: (i, 0))
    in_specs = [
        row_spec(D_MODEL),
        _const_spec((None, 1, D_MODEL), lsel3),
        _const_spec((None, D_MODEL, D_IN), lsel3),
        _const_spec((None, 1, D_IN), lsel3),
        _const_spec((None, len(POOL_WINDOWS), POOL_GW, POOL_OUT_GW), (layer, 0, 0, 0)),
        _const_spec((None, 1, D_MODEL), lsel3),
        _const_spec((None, CONV_WIDTH, CONV_DIM), lsel3),
        _const_spec((None, 1, CONV_DIM), lsel3),
        _const_spec((None, 1, CONV_DIM), lsel3),
        _const_spec((None, 1, CONV_DIM), lsel3),
        _const_spec((None, CONV_DIM, D_MODEL), lsel3),
        pl.BlockSpec((None, sb, POOL_HIST, POOL_WIDTH), lambda i: (layer, i, 0, 0)),
        pl.BlockSpec((None, sb, CONV_HIST, CONV_DIM), lambda i: (layer, i, 0, 0)),
    ]
    out_shape = [
        jax.ShapeDtypeStruct((rows_all, D_MODEL), F32),
        jax.ShapeDtypeStruct((rows_all, D_MODEL), F32),
        jax.ShapeDtypeStruct((rows_all, D_MODEL), F32),
        jax.ShapeDtypeStruct((rows_all, KV_DIM), F32),
        jax.ShapeDtypeStruct((rows_all, KV_DIM), F32),
        jax.ShapeDtypeStruct((depth, n_seq, POOL_HIST, POOL_WIDTH), F32),
        jax.ShapeDtypeStruct((depth, n_seq, CONV_HIST, CONV_DIM), F32),
    ]
    out_specs = [
        row_spec(D_MODEL), row_spec(D_MODEL), row_spec(D_MODEL), row_spec(KV_DIM), row_spec(KV_DIM),
        pl.BlockSpec((None, sb, POOL_HIST, POOL_WIDTH), lambda i: (layer, i, 0, 0)),
        pl.BlockSpec((None, sb, CONV_HIST, CONV_DIM), lambda i: (layer, i, 0, 0)),
    ]
    scratch = [
        pltpu.VMEM((sb, POOL_CARRY + tdec, POOL_WIDTH), F32),
        pltpu.VMEM((sb, CONV_CARRY + tdec, CONV_DIM), F32),
    ]
    args = [hs, norm_mix, w_in, b_in, pool_w, pool_scale, conv_w, conv_b, conv_ng, conv_nb,
            w_conv_out, state_pool, state_conv]
    aliases = {len(args) - 2: 5, len(args) - 1: 6}
    return pl.pallas_call(
        functools.partial(_sample_in_body, sb=sb, tdec=tdec),
        grid=(n_seq // sb,),
        in_specs=in_specs,
        out_specs=out_specs,
        out_shape=out_shape,
        input_output_aliases=aliases,
        scratch_shapes=scratch,
        compiler_params=pltpu.CompilerParams(
            dimension_semantics=("arbitrary",), vmem_limit_bytes=VMEM_LIMIT),
        name=f"sample_in_l{layer}",
    )(*args)


def _sample_attn_body(q_ref, kn_ref, vn_ref, ck_ref, cv_ref, bias_ref, sink_ref,
                      o_ref, ko_ref, vo_ref, qbd, kx, vx, *, sb, tdec):
    n_keys = WINDOW + tdec

    @pl.when(pl.program_id(0) == 0)
    def _():
        qbd[...] = jnp.zeros(qbd.shape, F32)
        kx[...] = jnp.zeros(kx.shape, F32)
        vx[...] = jnp.zeros(vx.shape, F32)

    def stage_qk(s):
        r0 = s * tdec
        qs = q_ref[r0:r0 + tdec, :]
        kn = kn_ref[r0:r0 + tdec, :]
        vn = vn_ref[r0:r0 + tdec, :]
        kc = ck_ref[s]
        vc = cv_ref[s]
        for h in range(N_HEADS):
            g_kv = h // GQA_GROUP
            qbd[s, h * tdec:(h + 1) * tdec, g_kv * HEAD_DIM:(g_kv + 1) * HEAD_DIM] = (
                qs[:, h * HEAD_DIM:(h + 1) * HEAD_DIM])
        kx[s, 0:WINDOW, :] = kc
        kx[s, WINDOW:n_keys, :] = kn
        vx[s, 0:WINDOW, :] = vc
        vx[s, WINDOW:n_keys, :] = vn
        ko_ref[s, 0:WINDOW - tdec, :] = kc[tdec:, :]
        ko_ref[s, WINDOW - tdec:WINDOW, :] = kn
        vo_ref[s, 0:WINDOW - tdec, :] = vc[tdec:, :]
        vo_ref[s, WINDOW - tdec:WINDOW, :] = vn
        return _dot_nt(qbd[s].astype(BF16), kx[s].astype(BF16))

    def stage_pv(s, p):
        r0 = s * tdec
        o_full = _dot(p, vx[s].astype(BF16))
        for h in range(N_HEADS):
            g_kv = h // GQA_GROUP
            o_ref[r0:r0 + tdec, h * HEAD_DIM:(h + 1) * HEAD_DIM] = (
                o_full[h * tdec:(h + 1) * tdec, g_kv * HEAD_DIM:(g_kv + 1) * HEAD_DIM])

    raw = [stage_qk(s) for s in range(sb)]
    probs = [_softmax_with_sink(r + bias_ref[...], sink_ref[...]).astype(BF16) for r in raw]
    for s in range(sb):
        stage_pv(s, probs[s])


def _sample_attn(layer, q, kn, vn, cache_k, cache_v, bias_s, sink_col, sb, tdec):
    depth = cache_k.shape[0]
    rows_all = q.shape[0]
    n_seq = rows_all // tdec
    rows = sb * tdec
    key_pad = 2 * WINDOW
    row_spec = lambda w: pl.BlockSpec((rows, w), lambda i: (i, 0))
    cache_spec = pl.BlockSpec((None, sb, WINDOW, KV_DIM), lambda i: (layer, i, 0, 0))
    in_specs = [
        row_spec(D_MODEL), row_spec(KV_DIM), row_spec(KV_DIM), cache_spec, cache_spec,
        _const_spec((N_HEADS * tdec, key_pad), (0, 0)),
        _const_spec((None, N_HEADS * tdec, 1), (layer, 0, 0)),
    ]
    args = [q, kn, vn, cache_k, cache_v, bias_s, sink_col]
    aliases = {3: 1, 4: 2}
    stacked = jax.ShapeDtypeStruct((depth, n_seq, WINDOW, KV_DIM), F32)
    return pl.pallas_call(
        functools.partial(_sample_attn_body, sb=sb, tdec=tdec),
        grid=(n_seq // sb,),
        in_specs=in_specs,
        out_specs=[row_spec(D_MODEL), cache_spec, cache_spec],
        out_shape=[jax.ShapeDtypeStruct((rows_all, D_MODEL), F32), stacked, stacked],
        input_output_aliases=aliases,
        scratch_shapes=[
            pltpu.VMEM((sb, N_HEADS * tdec, KV_DIM), F32),
            pltpu.VMEM((sb, key_pad, KV_DIM), F32),
            pltpu.VMEM((sb, key_pad, KV_DIM), F32),
        ],
        compiler_params=pltpu.CompilerParams(
            dimension_semantics=("arbitrary",), vmem_limit_bytes=VMEM_LIMIT),
        name=f"sample_attn_l{layer}",
    )(*args)


def _sample_out_body(h_ref, mrg_ref, g2_ref, o_ref, wao_ref, wo_ref, g_ref, wup_ref, wdn_ref,
                     gf_ref, out_ref, *, final_norm):
    yc = _dot(o_ref[...].astype(BF16), wao_ref[...])
    merged = mrg_ref[...] + g2_ref[...] * yc
    h1 = h_ref[...] + _dot(merged.astype(BF16), wo_ref[...])
    y = _mlp_rows(h1, g_ref, wup_ref, wdn_ref)
    if final_norm:
        y = _rms(y, gf_ref[...])
    out_ref[...] = y


def _sample_out(layer, hs, mrg, g2, o, tm, w_attn_out, w_out, norm_mlp, w_up, w_down,
                norm_final, final_norm):
    rows = hs.shape[0]
    lsel3 = (layer, 0, 0)
    row_spec = pl.BlockSpec((tm, D_MODEL), lambda i: (i, 0))
    return pl.pallas_call(
        functools.partial(_sample_out_body, final_norm=final_norm),
        grid=(rows // tm,),
        in_specs=[
            row_spec, row_spec, row_spec, row_spec,
            _const_spec((None, D_MODEL, D_MODEL), lsel3),
            _const_spec((None, D_MODEL, D_MODEL), lsel3),
            _const_spec((None, 1, D_MODEL), lsel3),
            _const_spec((None, D_MODEL, D_FF), lsel3),
            _const_spec((None, D_FF, D_MODEL), lsel3),
            _const_spec((1, D_MODEL), (0, 0)),
        ],
        out_specs=row_spec,
        out_shape=jax.ShapeDtypeStruct((rows, D_MODEL), F32),
        compiler_params=pltpu.CompilerParams(
            dimension_semantics=("arbitrary",), vmem_limit_bytes=VMEM_LIMIT),
        name=f"sample_out_l{layer}",
    )(hs, mrg, g2, o, w_attn_out, w_out, norm_mlp, w_up, w_down, norm_final)


def _forward(x_prompt, x_sample, state_pool, state_conv, cache_k, cache_v, meta_tokens,
             rel_bias, norm_mix, w_in, b_in, pool_w, pool_scale, conv_w, conv_b, conv_norm_g,
             conv_norm_b, w_conv_out, attn_sinks, w_attn_out, w_out, norm_mlp, w_up, w_down,
             norm_final, *, past_len, tm_prompt, tm_sample, sb_in, sb_attn):
    depth = w_in.shape[0]
    nb, seq, _ = x_prompt.shape
    n_seq, tdec, _ = x_sample.shape
    assert past_len >= POOL_HIST and past_len >= WINDOW
    t_valid = seq + N_META
    tp = -(-t_valid // tm_prompt) * tm_prompt

    meta = jnp.broadcast_to(meta_tokens[None].astype(F32), (nb, N_META, D_MODEL))
    hp = jnp.concatenate(
        [meta, x_prompt, jnp.zeros((nb, tp - t_valid, D_MODEL), F32)], axis=1)
    hs = x_sample.reshape(n_seq * tdec, D_MODEL)

    row3 = lambda a: a.reshape(depth, 1, a.shape[-1])
    w_in_b, pool_w_b, wco_b = w_in.astype(BF16), pool_w.astype(BF16), w_conv_out.astype(BF16)
    wao_b, wo_b = w_attn_out.astype(BF16), w_out.astype(BF16)
    w_up_b, w_down_b = w_up.astype(BF16), w_down.astype(BF16)
    nm3, bin3, ps3 = row3(norm_mix), row3(b_in), row3(pool_scale)
    cb3, cng3, cnb3, nmlp3 = row3(conv_b), row3(conv_norm_g), row3(conv_norm_b), row3(norm_mlp)
    nf2 = norm_final.reshape(1, D_MODEL)

    band = _banded_bias(rel_bias)
    own_only = jnp.asarray(np.arange(2 * WINDOW) >= WINDOW)
    bias_tab = jnp.stack([band, jnp.where(own_only, band, -jnp.inf)])
    bias_tab = bias_tab.reshape(2, N_KV_HEADS, GQA_GROUP, WINDOW, 2 * WINDOW)
    bias_tab = bias_tab.transpose(0, 1, 4, 2, 3).reshape(
        2, N_KV_HEADS, 2 * WINDOW, GQA_GROUP * WINDOW)
    bias_s = band[:, :tdec, :].reshape(N_HEADS * tdec, 2 * WINDOW)
    sink_col = jnp.repeat(attn_sinks.astype(F32), tdec, axis=1).reshape(depth, N_HEADS * tdec, 1)

    cache_k2 = cache_k.reshape(depth, n_seq, WINDOW, KV_DIM)
    cache_v2 = cache_v.reshape(depth, n_seq, WINDOW, KV_DIM)
    sinks = attn_sinks.astype(F32)

    pool_p, conv_p, k_p, v_p = [], [], [], []
    pool_s, conv_s, k_s, v_s = state_pool, state_conv, cache_k2, cache_v2
    for l in range(depth):
        last = l == depth - 1
        hp, pst, cst, kst, vst = _prompt_mixer(
            l, hp, t_valid, tm_prompt, sinks, nm3, w_in_b, bin3, pool_w_b, ps3, conv_w, cb3,
            cng3, cnb3, wco_b, bias_tab, wao_b, wo_b)
        hp = _mlp(l, hp.reshape(nb * tp, D_MODEL), tm_prompt, nmlp3, w_up_b, w_down_b, nf2,
                  last).reshape(nb, tp, D_MODEL)
        pool_p.append(pst[:, POOL_CARRY - POOL_HIST:])
        conv_p.append(cst[:, CONV_CARRY - CONV_HIST:])
        k_p.append(kst.reshape(nb, WINDOW, N_KV_HEADS, HEAD_DIM))
        v_p.append(vst.reshape(nb, WINDOW, N_KV_HEADS, HEAD_DIM))

        mrg, g2, q, kn, vn, pool_s, conv_s = _sample_in(
            l, hs, sb_in, tdec, nm3, w_in_b, bin3, pool_w_b, ps3, conv_w, cb3, cng3, cnb3,
            wco_b, pool_s, conv_s)
        o, k_s, v_s = _sample_attn(l, q, kn, vn, k_s, v_s, bias_s, sink_col, sb_attn, tdec)
        hs = _sample_out(l, hs, mrg, g2, o, tm_sample, wao_b, wo_b, nmlp3, w_up_b, w_down_b,
                         nf2, last)

    y_prompt = hp[:, N_META:t_valid]
    y_sample = hs.reshape(n_seq, tdec, D_MODEL)
    kv_shape = (depth, n_seq, WINDOW, N_KV_HEADS, HEAD_DIM)
    return (y_prompt, y_sample, jnp.stack(pool_p), pool_s, jnp.stack(conv_p), conv_s,
            jnp.stack(k_p), k_s.reshape(kv_shape), jnp.stack(v_p), v_s.reshape(kv_shape))


def kernel(x_prompt, x_sample, state_pool, state_conv, cache_k, cache_v, meta_tokens, rel_bias, norm_mix, w_in, b_in, pool_w, pool_scale, conv_w, conv_b, conv_norm_g, conv_norm_b, w_conv_out, attn_sinks, w_attn_out, w_out, norm_mlp, w_up, w_down, norm_final):
    return _forward(
        x_prompt, x_sample, state_pool, state_conv, cache_k, cache_v, meta_tokens, rel_bias,
        norm_mix, w_in, b_in, pool_w, pool_scale, conv_w, conv_b, conv_norm_g, conv_norm_b,
        w_conv_out, attn_sinks, w_attn_out, w_out, norm_mlp, w_up, w_down, norm_final,
        past_len=8192, tm_prompt=640, tm_sample=512, sb_in=32, sb_attn=8)
```

```python
import functools

import numpy as np
import jax
import jax.numpy as jnp
from jax import lax
from jax.experimental import pallas as pl
from jax.experimental.pallas import tpu as pltpu

F32 = jnp.float32
BF16 = jnp.bfloat16

D_MODEL = 1024
N_META = 16
POOL_WINDOWS = (2, 4, 8, 16)
POOL_WIDTH = 512
POOL_GW = 128
POOL_OUT_GW = 256
POOL_HIST = 15
CONV_DIM = 512
CONV_WIDTH = 31
CONV_HIST = 30
HEAD_DIM = 64
N_HEADS = 16
N_KV_HEADS = 4
GQA_GROUP = 4
KV_DIM = N_KV_HEADS * HEAD_DIM
WINDOW = 128
N_BUCKETS = 32
MAX_DISTANCE = 128
D_FF = 4096
NORM_EPS = 1e-6
C_POOL = 0
C_GLU = C_POOL + POOL_WIDTH
C_Q = C_GLU + 2 * CONV_DIM
C_K = C_Q + N_HEADS * HEAD_DIM
C_V = C_K + KV_DIM
C_GATE = C_V + KV_DIM
D_IN = C_GATE + 3 * D_MODEL

SUBLANES = 8
LANES = 128
POOL_CARRY = 16
CONV_CARRY = 32
FF_CHUNK = 1024
CONV_ROWS = 128
NORM_ROWS = 32
VMEM_LIMIT = 60 * 1024 * 1024


def _dot(a, b):
    return jnp.dot(a, b, preferred_element_type=F32)


def _dot_nt(a, b):
    return lax.dot_general(a, b, (((1,), (1,)), ((), ())), preferred_element_type=F32)


def _rms(x, g):
    ms = jnp.mean(x * x, axis=-1, keepdims=True)
    return x * lax.rsqrt(ms + NORM_EPS) * g


def _layernorm_silu(y, g, b):
    mu = jnp.mean(y, axis=-1, keepdims=True)
    yc = y - mu
    var = jnp.mean(yc * yc, axis=-1, keepdims=True)
    yn = yc * lax.rsqrt(var + NORM_EPS) * g + b
    return yn * jax.nn.sigmoid(yn)


def _softmax_with_sink(logits, sink):
    m = jnp.maximum(jnp.max(logits, axis=-1, keepdims=True), sink)
    p = jnp.exp(logits - m)
    denom = jnp.sum(p, axis=-1, keepdims=True) + jnp.exp(sink - m)
    return p * (1.0 / denom)


def _const_spec(block_shape, index):
    return pl.BlockSpec(block_shape, lambda *_: index, pipeline_mode=pl.Buffered(1))


def _t5_bucket(dist):
    n = np.maximum(dist, 0)
    exact = N_BUCKETS // 2
    large = exact + (np.log(np.maximum(n, 1) / exact) / np.log(MAX_DISTANCE / exact)
                     * (N_BUCKETS - exact)).astype(np.int32)
    large = np.minimum(large, N_BUCKETS - 1)
    return np.where(n < exact, n, large).astype(np.int32)


def _banded_bias(rel_bias):
    n_diag = 3 * WINDOW - 1
    per_dist = rel_bias[_t5_bucket(np.arange(WINDOW + 1))].astype(F32).T
    profile = jnp.full((N_HEADS, n_diag + 1), -jnp.inf, F32)
    profile = profile.at[:, WINDOW - 1:2 * WINDOW].set(per_dist[:, ::-1])
    skew = jnp.tile(profile, (1, WINDOW))[:, :WINDOW * n_diag].reshape(N_HEADS, WINDOW, n_diag)
    return skew[:, :, WINDOW - 1:WINDOW - 1 + 2 * WINDOW]


def _prompt_mixer_body(sinks_ref, h_ref, nm_ref, win_ref, bin_ref, pw_ref, ps_ref, cw_ref,
                       cb_ref, cng_ref, cnb_ref, wco_ref, bias_ref, wao_ref, wo_ref,
                       ho_ref, pst_ref, cst_ref, kst_ref, vst_ref,
                       u_scr, exta, extc, kbuf, vbuf, q_scr, o_scr, s_scr, mrg, xsh, g1_scr, g2_scr,
                       y_scr, *, layer, tm, e_loc):
    t = pl.program_id(1)

    @pl.when(t == 0)
    def _():
        exta[0:POOL_CARRY, :] = jnp.zeros((POOL_CARRY, POOL_WIDTH), F32)
        extc[0:CONV_CARRY, :] = jnp.zeros((CONV_CARRY, CONV_DIM), F32)
        kbuf[0:WINDOW, :] = jnp.zeros((WINDOW, KV_DIM), BF16)
        vbuf[:, 0:WINDOW] = jnp.zeros((KV_DIM, WINDOW), BF16)

    @pl.when(t > 0)
    def _():
        exta[0:POOL_CARRY, :] = exta[tm:tm + POOL_CARRY, :]
        extc[0:CONV_CARRY, :] = extc[tm:tm + CONV_CARRY, :]
        kbuf[0:WINDOW, :] = kbuf[tm:tm + WINDOW, :]
        vbuf[:, 0:WINDOW] = vbuf[:, tm:tm + WINDOW]

    u_scr[...] = _rms(h_ref[...], nm_ref[...]).astype(BF16)

    def proj(c0, width):
        return _dot(u_scr[...], win_ref[:, c0:c0 + width]) + bin_ref[:, c0:c0 + width]

    def gate(i):
        return jax.nn.sigmoid(proj(C_GATE + i * D_MODEL, D_MODEL))

    c = proj(C_GLU, 2 * CONV_DIM)
    extc[CONV_CARRY:CONV_CARRY + tm, :] = c[:, :CONV_DIM] * jax.nn.sigmoid(c[:, CONV_DIM:])
    a = proj(C_POOL, POOL_WIDTH)
    exta[POOL_CARRY:POOL_CARRY + tm, :] = a

    def project_q():
        q_scr[...] = (proj(C_Q, N_HEADS * HEAD_DIM) * (HEAD_DIM ** -0.5)).astype(BF16)

    def project_kv():
        kv = proj(C_K, 2 * KV_DIM)
        kbuf[WINDOW:WINDOW + tm, :] = kv[:, :KV_DIM].astype(BF16)
        vbuf[:, WINDOW:WINDOW + tm] = kv[:, KV_DIM:].T.astype(BF16)
        kst_ref[...] = kv[e_loc - WINDOW:e_loc, :KV_DIM]
        vst_ref[...] = kv[e_loc - WINDOW:e_loc, KV_DIM:]

    def project_gate(i, dst):
        dst[...] = gate(i)

    pst_ref[...] = exta[e_loc:e_loc + POOL_CARRY, :]
    cst_ref[...] = extc[e_loc:e_loc + CONV_CARRY, :]

    first = CONV_CARRY - CONV_HIST
    last_off = first + CONV_WIDTH - 1

    def conv_chunk(ci):
        r0 = ci * CONV_ROWS
        for li in range(CONV_DIM // LANES):
            c0 = li * LANES
            slot = ((ci * (CONV_DIM // LANES) + li) % 2) * SUBLANES
            acc = jnp.zeros((CONV_ROWS, LANES), F32)
            for sh in range(SUBLANES):
                taps = [a0 for a0 in range(0, last_off + 1, SUBLANES) if first <= a0 + sh <= last_off]
                n = taps[-1] + CONV_ROWS
                if sh:
                    xsh[slot + sh, 0:n, :] = extc[r0 + sh:r0 + sh + n, c0:c0 + LANES]
                for a0 in taps:
                    w_row = cw_ref[a0 + sh - first:a0 + sh - first + 1, c0:c0 + LANES]
                    if sh:
                        acc = acc + xsh[slot + sh, a0:a0 + CONV_ROWS, :] * w_row
                    else:
                        acc = acc + extc[r0 + a0:r0 + a0 + CONV_ROWS, c0:c0 + LANES] * w_row
            y_scr[r0:r0 + CONV_ROWS, c0:c0 + LANES] = acc + cb_ref[:, c0:c0 + LANES]
        for r1 in range(r0, r0 + CONV_ROWS, NORM_ROWS):
            s_scr[r1:r1 + NORM_ROWS, :] = _layernorm_silu(
                y_scr[r1:r1 + NORM_ROWS, :], cng_ref[...], cnb_ref[...]).astype(BF16)

    def attn_qk(qb):
        r0 = qb * WINDOW
        raw = []
        for g_kv in range(N_KV_HEADS):
            heads = range(g_kv * GQA_GROUP, (g_kv + 1) * GQA_GROUP)
            q4 = jnp.concatenate(
                [q_scr[r0:r0 + WINDOW, h * HEAD_DIM:(h + 1) * HEAD_DIM] for h in heads], axis=0)
            kw = kbuf[r0:r0 + 2 * WINDOW, g_kv * HEAD_DIM:(g_kv + 1) * HEAD_DIM]
            raw.append(_dot_nt(kw, q4))
        return raw

    def attn_softmax(qb, raw):
        sel = jnp.where(t == 0, 1, 0) if qb == 0 else 0
        probs = []
        for g_kv in range(N_KV_HEADS):
            heads = range(g_kv * GQA_GROUP, (g_kv + 1) * GQA_GROUP)
            logits = raw[g_kv] + bias_ref[sel, g_kv]
            sink = jnp.concatenate(
                [jnp.full((1, WINDOW), sinks_ref[layer, h], F32) for h in heads], axis=1)
            m = jnp.maximum(jnp.max(logits, axis=0, keepdims=True), sink)
            p = jnp.exp(logits - m)
            denom = jnp.sum(p, axis=0, keepdims=True) + jnp.exp(sink - m)
            probs.append((p.astype(BF16), 1.0 / denom))
        return probs

    def attn_pv(qb, probs):
        r0 = qb * WINDOW
        outs = []
        for g_kv in range(N_KV_HEADS):
            p, inv_denom = probs[g_kv]
            vw_t = vbuf[g_kv * HEAD_DIM:(g_kv + 1) * HEAD_DIM, r0:r0 + 2 * WINDOW]
            o_t = _dot(vw_t, p) * inv_denom
            outs.extend(o_t[:, g * WINDOW:(g + 1) * WINDOW] for g in range(GQA_GROUP))
        o_scr[r0:r0 + WINDOW, :] = jnp.concatenate(outs, axis=0).T.astype(BF16)

    def pool_group(gi):
        win = POOL_WINDOWS[gi]
        pos = t * tm + lax.broadcasted_iota(jnp.int32, (tm, 1), 0)
        c0 = gi * POOL_GW
        own = exta[POOL_CARRY:POOL_CARRY + tm, c0:c0 + POOL_GW]
        s = own
        for j in range(1, win):
            s = s + exta[POOL_CARRY - j:POOL_CARRY - j + tm, c0:c0 + POOL_GW]
        cnt = jnp.minimum(pos + 1, win).astype(F32)
        r = (s / cnt - own).astype(BF16)
        o0 = gi * POOL_OUT_GW
        ya = _dot(r, pw_ref[gi]) * ps_ref[:, o0:o0 + POOL_OUT_GW]
        mrg[:, o0:o0 + POOL_OUT_GW] = mrg[:, o0:o0 + POOL_OUT_GW] * ya

    vector_jobs = [functools.partial(conv_chunk, i) for i in range(tm // CONV_ROWS)]
    matmul_jobs = [functools.partial(project_gate, 0, mrg), functools.partial(project_gate, 1, g1_scr),
                   functools.partial(project_gate, 2, g2_scr)]
    matmul_jobs += [functools.partial(pool_group, gi) for gi in range(len(POOL_WINDOWS))]

    def run_next(jobs):
        if jobs:
            jobs.pop(0)()

    run_next(vector_jobs)
    project_q()
    run_next(vector_jobs)
    project_kv()
    n_blocks = tm // WINDOW
    raw = attn_qk(0)
    run_next(matmul_jobs)
    run_next(vector_jobs)
    for i in range(n_blocks):
        raw_next = attn_qk(i + 1) if i + 1 < n_blocks else None
        probs = attn_softmax(i, raw)
        run_next(matmul_jobs)
        attn_pv(i, probs)
        run_next(vector_jobs)
        raw = raw_next
    while vector_jobs or matmul_jobs:
        run_next(vector_jobs)
        run_next(matmul_jobs)

    yb = _dot(s_scr[...], wco_ref[...])
    mrg[...] = mrg[...] + g1_scr[...] * yb
    yc = _dot(o_scr[...], wao_ref[...])
    merged = mrg[...] + g2_scr[...] * yc
    ho_ref[...] = h_ref[...] + _dot(merged.astype(BF16), wo_ref[...])


def _prompt_mixer_tokens_body(sinks_ref, x_ref, prev_ref, meta_ref, *rest, t_valid, **kw):
    h_scr = rest[-1]
    tm = kw["tm"]
    t = pl.program_id(1)
    head = jnp.where(t == 0, meta_ref[...], prev_ref[...])
    tile = jnp.concatenate([head, x_ref[0:tm - N_META, :]], axis=0)
    row = t * tm + lax.broadcasted_iota(jnp.int32, (tm, 1), 0)
    h_scr[...] = jnp.where(row < t_valid, tile, 0.0)
    _prompt_mixer_body(sinks_ref, h_scr, *rest[:-1], **kw)


def _prompt_mixer(layer, hp, tokens, t_valid, tm, sinks, norm_mix, w_in, b_in, pool_w, pool_scale,
                  conv_w, conv_b, conv_ng, conv_nb, w_conv_out, bias_tab, w_attn_out, w_out):
    nb = tokens[0].shape[0] if hp is None else hp.shape[0]
    n_tiles = -(-t_valid // tm)
    tp = n_tiles * tm
    e_loc = t_valid - (n_tiles - 1) * tm
    assert e_loc >= WINDOW and e_loc % SUBLANES == 0 and tm % N_META == 0
    lsel3 = (layer, 0, 0)
    if hp is None:
        x_prompt, meta = tokens
        x_blocks = -(-x_prompt.shape[1] // tm)
        per_tile = tm // N_META
        body = functools.partial(_prompt_mixer_tokens_body, layer=layer, tm=tm, e_loc=e_loc,
                                 t_valid=t_valid)
        h_args = [x_prompt, x_prompt, meta]
        h_specs = [
            pl.BlockSpec((None, tm, D_MODEL), lambda b, t: (b, jnp.minimum(t, x_blocks - 1), 0)),
            pl.BlockSpec((None, N_META, D_MODEL),
                         lambda b, t: (b, jnp.maximum(t * per_tile - 1, 0), 0)),
            _const_spec((N_META, D_MODEL), (0, 0)),
        ]
        extra_scratch = [pltpu.VMEM((tm, D_MODEL), F32)]
    else:
        body = functools.partial(_prompt_mixer_body, layer=layer, tm=tm, e_loc=e_loc)
        h_args = [hp]
        h_specs = [pl.BlockSpec((None, tm, D_MODEL), lambda b, t: (b, t, 0))]
        extra_scratch = []
    in_specs = [pl.BlockSpec(memory_space=pltpu.SMEM)] + h_specs + [
        _const_spec((None, 1, D_MODEL), lsel3),
        _const_spec((None, D_MODEL, D_IN), lsel3),
        _const_spec((None, 1, D_IN), lsel3),
        _const_spec((None, len(POOL_WINDOWS), POOL_GW, POOL_OUT_GW), (layer, 0, 0, 0)),
        _const_spec((None, 1, D_MODEL), lsel3),
        _const_spec((None, CONV_WIDTH, CONV_DIM), lsel3),
        _const_spec((None, 1, CONV_DIM), lsel3),
        _const_spec((None, 1, CONV_DIM), lsel3),
        _const_spec((None, 1, CONV_DIM), lsel3),
        _const_spec((None, CONV_DIM, D_MODEL), lsel3),
        _const_spec((2, N_KV_HEADS, 2 * WINDOW, GQA_GROUP * WINDOW), (0, 0, 0, 0)),
        _const_spec((None, D_MODEL, D_MODEL), lsel3),
        _const_spec((None, D_MODEL, D_MODEL), lsel3),
    ]
    out_shape = [
        jax.ShapeDtypeStruct((nb, tp, D_MODEL), F32),
        jax.ShapeDtypeStruct((nb, POOL_CARRY, POOL_WIDTH), F32),
        jax.ShapeDtypeStruct((nb, CONV_CARRY, CONV_DIM), F32),
        jax.ShapeDtypeStruct((nb, WINDOW, KV_DIM), F32),
        jax.ShapeDtypeStruct((nb, WINDOW, KV_DIM), F32),
    ]
    out_specs = [
        pl.BlockSpec((None, tm, D_MODEL), lambda b, t: (b, t, 0)),
        pl.BlockSpec((None, POOL_CARRY, POOL_WIDTH), lambda b, t: (b, 0, 0)),
        pl.BlockSpec((None, CONV_CARRY, CONV_DIM), lambda b, t: (b, 0, 0)),
        pl.BlockSpec((None, WINDOW, KV_DIM), lambda b, t: (b, 0, 0)),
        pl.BlockSpec((None, WINDOW, KV_DIM), lambda b, t: (b, 0, 0)),
    ]
    scratch = [
        pltpu.VMEM((tm, D_MODEL), BF16),
        pltpu.VMEM((POOL_CARRY + tm, POOL_WIDTH), F32),
        pltpu.VMEM((CONV_CARRY + tm, CONV_DIM), F32),
        pltpu.VMEM((WINDOW + tm, KV_DIM), BF16),
        pltpu.VMEM((KV_DIM, WINDOW + tm), BF16),
        pltpu.VMEM((tm, D_MODEL), BF16),
        pltpu.VMEM((tm, D_MODEL), BF16),
        pltpu.VMEM((tm, CONV_DIM), BF16),
        pltpu.VMEM((tm, D_MODEL), F32),
        pltpu.VMEM((2 * SUBLANES, CONV_ROWS + CONV_CARRY, LANES), F32),
        pltpu.VMEM((tm, D_MODEL), F32),
        pltpu.VMEM((tm, D_MODEL), F32),
        pltpu.VMEM((tm, CONV_DIM), F32),
    ]
    return pl.pallas_call(
        body,
        grid=(nb, n_tiles),
        in_specs=in_specs,
        out_specs=out_specs,
        out_shape=out_shape,
        scratch_shapes=scratch + extra_scratch,
        compiler_params=pltpu.CompilerParams(
            dimension_semantics=("arbitrary", "arbitrary"), vmem_limit_bytes=VMEM_LIMIT),
        name=f"prompt_mixer_l{layer}",
    )(sinks, *h_args, norm_mix, w_in, b_in, pool_w, pool_scale, conv_w, conv_b, conv_ng, conv_nb,
      w_conv_out, bias_tab, w_attn_out, w_out)


def _mlp_rows(x, g_ref, wup_ref, wdn_ref):
    u = _rms(x, g_ref[...]).astype(BF16)
    acc = x
    for c0 in range(0, D_FF, FF_CHUNK):
        up = jnp.maximum(_dot(u, wup_ref[:, c0:c0 + FF_CHUNK]), 0.0)
        acc = acc + _dot((up * up).astype(BF16), wdn_ref[c0:c0 + FF_CHUNK, :])
    return acc


def _mlp_body(h_ref, g_ref, wup_ref, wdn_ref, gf_ref, o_ref, *, final_norm):
    y = _mlp_rows(h_ref[...], g_ref, wup_ref, wdn_ref)
    if final_norm:
        y = _rms(y, gf_ref[...])
    o_ref[...] = y


def _mlp(layer, h2d, tm, norm_mlp, w_up, w_down, norm_final, final_norm):
    rows = h2d.shape[0]
    lsel3 = (layer, 0, 0)
    return pl.pallas_call(
        functools.partial(_mlp_body, final_norm=final_norm),
        grid=(rows // tm,),
        in_specs=[
            pl.BlockSpec((tm, D_MODEL), lambda i: (i, 0)),
            _const_spec((None, 1, D_MODEL), lsel3),
            _const_spec((None, D_MODEL, D_FF), lsel3),
            _const_spec((None, D_FF, D_MODEL), lsel3),
            _const_spec((1, D_MODEL), (0, 0)),
        ],
        out_specs=pl.BlockSpec((tm, D_MODEL), lambda i: (i, 0)),
        out_shape=jax.ShapeDtypeStruct((rows, D_MODEL), F32),
        compiler_params=pltpu.CompilerParams(
            dimension_semantics=("arbitrary",), vmem_limit_bytes=VMEM_LIMIT),
        name=f"mlp_l{layer}",
    )(h2d, norm_mlp, w_up, w_down, norm_final)


def _mlp_final_body(h_ref, g_ref, wup_ref, wdn_ref, gf_ref, o_ref, cur, prev, *, tm, n_tiles):
    t = pl.program_id(1)

    @pl.when(t > 0)
    def _():
        prev[...] = cur[...]

    @pl.when(t < n_tiles)
    def _():
        cur[...] = _rms(_mlp_rows(h_ref[...], g_ref, wup_ref, wdn_ref), gf_ref[...])

    @pl.when(t > 0)
    def _():
        o_ref[0:tm - N_META, :] = prev[N_META:tm, :]
        o_ref[tm - N_META:tm, :] = cur[0:N_META, :]


def _mlp_final(layer, hp, seq, tm, norm_mlp, w_up, w_down, norm_final):
    nb, tp, _ = hp.shape
    n_tiles = tp // tm
    n_out = -(-seq // tm)
    lsel3 = (layer, 0, 0)
    return pl.pallas_call(
        functools.partial(_mlp_final_body, tm=tm, n_tiles=n_tiles),
        grid=(nb, n_out + 1),
        in_specs=[
            pl.BlockSpec((None, tm, D_MODEL), lambda b, t: (b, jnp.minimum(t, n_tiles - 1), 0)),
            _const_spec((None, 1, D_MODEL), lsel3),
            _const_spec((None, D_MODEL, D_FF), lsel3),
            _const_spec((None, D_FF, D_MODEL), lsel3),
            _const_spec((1, D_MODEL), (0, 0)),
        ],
        out_specs=pl.BlockSpec((None, tm, D_MODEL), lambda b, t: (b, jnp.maximum(t - 1, 0), 0)),
        out_shape=jax.ShapeDtypeStruct((nb, seq, D_MODEL), F32),
        scratch_shapes=[pltpu.VMEM((tm, D_MODEL), F32), pltpu.VMEM((tm, D_MODEL), F32)],
        compiler_params=pltpu.CompilerParams(
            dimension_semantics=("arbitrary", "arbitrary"), vmem_limit_bytes=VMEM_LIMIT),
        name=f"mlp_final_l{layer}",
    )(hp, norm_mlp, w_up, w_down, norm_final)


def _sample_in_body(h_ref, nm_ref, win_ref, bin_ref, pw_ref, ps_ref, cw_ref, cb_ref, cng_ref,
                    cnb_ref, wco_ref, sp_ref, sc_ref,
                    mrg_ref, g2_ref, q_ref, kn_ref, vn_ref, po_ref, co_ref,
                    exta, extc, *, sb, tdec):
    rows = sb * tdec
    x = h_ref[...]
    u = _rms(x, nm_ref[...]).astype(BF16)

    def proj(c0, width):
        return _dot(u, win_ref[:, c0:c0 + width]) + bin_ref[:, c0:c0 + width]

    def gate(i):
        return jax.nn.sigmoid(proj(C_GATE + i * D_MODEL, D_MODEL))

    a = proj(C_POOL, POOL_WIDTH)
    exta[:, 1:POOL_CARRY, :] = sp_ref[...]
    exta[:, POOL_CARRY:POOL_CARRY + tdec, :] = a.reshape(sb, tdec, POOL_WIDTH)
    ya_parts = []
    for gi, win in enumerate(POOL_WINDOWS):
        c0 = gi * POOL_GW
        own = a[:, c0:c0 + POOL_GW]
        s = own
        for j in range(1, win):
            s = s + exta[:, POOL_CARRY - j:POOL_CARRY - j + tdec, c0:c0 + POOL_GW].reshape(
                rows, POOL_GW)
        r = (s / float(win) - own).astype(BF16)
        ya_parts.append(_dot(r, pw_ref[gi]))
    ya = jnp.concatenate(ya_parts, axis=1) * ps_ref[...]
    po_ref[...] = exta[:, POOL_CARRY + tdec - POOL_HIST:POOL_CARRY + tdec, :]
    merged = gate(0) * ya

    c = proj(C_GLU, 2 * CONV_DIM)
    glu = c[:, :CONV_DIM] * jax.nn.sigmoid(c[:, CONV_DIM:])
    first = CONV_CARRY - CONV_HIST
    extc[:, first:CONV_CARRY, :] = sc_ref[...]
    extc[:, CONV_CARRY:CONV_CARRY + tdec, :] = glu.reshape(sb, tdec, CONV_DIM)
    acc = jnp.zeros((rows, CONV_DIM), F32)
    for j in range(CONV_WIDTH):
        acc = acc + extc[:, first + j:first + j + tdec, :].reshape(rows, CONV_DIM) * cw_ref[j:j + 1, :]
    y = acc + cb_ref[...]
    s_act = _layernorm_silu(y, cng_ref[...], cnb_ref[...]).astype(BF16)
    co_ref[...] = extc[:, CONV_CARRY + tdec - CONV_HIST:CONV_CARRY + tdec, :]
    merged = merged + gate(1) * _dot(s_act, wco_ref[...])

    mrg_ref[...] = merged
    g2_ref[...] = gate(2)
    q_ref[...] = proj(C_Q, N_HEADS * HEAD_DIM) * (HEAD_DIM ** -0.5)
    kv = proj(C_K, 2 * KV_DIM)
    kn_ref[...] = kv[:, :KV_DIM]
    vn_ref[...] = kv[:, KV_DIM:]


def _sample_in(layer, hs, sb, tdec, norm_mix, w_in, b_in, pool_w, pool_scale, conv_w, conv_b,
               conv_ng, conv_nb, w_conv_out, state_pool, state_conv):
    depth = state_pool.shape[0]
    rows_all = hs.shape[0]
    n_seq = rows_all // tdec
    rows = sb * tdec
    lsel3 = (layer, 0, 0)
    row_spec = lambda w: pl.BlockSpec((rows, w), lambda i: (i, 0))
    in_specs = [
        row_spec(D_MODEL),
        _const_spec((None, 1, D_MODEL), lsel3),
        _const_spec((None, D_MODEL, D_IN), lsel3),
        _const_spec((None, 1, D_IN), lsel3),
        _const_spec((None, len(POOL_WINDOWS), POOL_GW, POOL_OUT_GW), (layer, 0, 0, 0)),
        _const_spec((None, 1, D_MODEL), lsel3),
        _const_spec((None, CONV_WIDTH, CONV_DIM), lsel3),
        _const_spec((None, 1, CONV_DIM), lsel3),
        _const_spec((None, 1, CONV_DIM), lsel3),
        _const_spec((None, 1, CONV_DIM), lsel3),
        _const_spec((None, CONV_DIM, D_MODEL), lsel3),
        pl.BlockSpec((None, sb, POOL_HIST, POOL_WIDTH), lambda i: (layer, i, 0, 0)),
        pl.BlockSpec((None, sb, CONV_HIST, CONV_DIM), lambda i: (layer, i, 0, 0)),
    ]
    out_shape = [
        jax.ShapeDtypeStruct((rows_all, D_MODEL), F32),
        jax.ShapeDtypeStruct((rows_all, D_MODEL), F32),
        jax.ShapeDtypeStruct((rows_all, D_MODEL), F32),
        jax.ShapeDtypeStruct((rows_all, KV_DIM), F32),
        jax.ShapeDtypeStruct((rows_all, KV_DIM), F32),
        jax.ShapeDtypeStruct((depth, n_seq, POOL_HIST, POOL_WIDTH), F32),
        jax.ShapeDtypeStruct((depth, n_seq, CONV_HIST, CONV_DIM), F32),
    ]
    out_specs = [
        row_spec(D_MODEL), row_spec(D_MODEL), row_spec(D_MODEL), row_spec(KV_DIM), row_spec(KV_DIM),
        pl.BlockSpec((None, sb, POOL_HIST, POOL_WIDTH), lambda i: (layer, i, 0, 0)),
        pl.BlockSpec((None, sb, CONV_HIST, CONV_DIM), lambda i: (layer, i, 0, 0)),
    ]
    scratch = [
        pltpu.VMEM((sb, POOL_CARRY + tdec, POOL_WIDTH), F32),
        pltpu.VMEM((sb, CONV_CARRY + tdec, CONV_DIM), F32),
    ]
    args = [hs, norm_mix, w_in, b_in, pool_w, pool_scale, conv_w, conv_b, conv_ng, conv_nb,
            w_conv_out, state_pool, state_conv]
    aliases = {len(args) - 2: 5, len(args) - 1: 6}
    return pl.pallas_call(
        functools.partial(_sample_in_body, sb=sb, tdec=tdec),
        grid=(n_seq // sb,),
        in_specs=in_specs,
        out_specs=out_specs,
        out_shape=out_shape,
        input_output_aliases=aliases,
        scratch_shapes=scratch,
        compiler_params=pltpu.CompilerParams(
            dimension_semantics=("arbitrary",), vmem_limit_bytes=VMEM_LIMIT),
        name=f"sample_in_l{layer}",
    )(*args)


def _sample_attn_body(q_ref, kn_ref, vn_ref, ck_ref, cv_ref, bias_ref, sink_ref,
                      o_ref, ko_ref, vo_ref, qbd, kx, vx, *, sb, tdec):
    n_keys = WINDOW + tdec

    @pl.when(pl.program_id(0) == 0)
    def _():
        qbd[...] = jnp.zeros(qbd.shape, F32)
        kx[...] = jnp.zeros(kx.shape, F32)
        vx[...] = jnp.zeros(vx.shape, F32)

    def stage_qk(s):
        r0 = s * tdec
        qs = q_ref[r0:r0 + tdec, :]
        kn = kn_ref[r0:r0 + tdec, :]
        vn = vn_ref[r0:r0 + tdec, :]
        kc = ck_ref[s]
        vc = cv_ref[s]
        for h in range(N_HEADS):
            g_kv = h // GQA_GROUP
            qbd[s, h * tdec:(h + 1) * tdec, g_kv * HEAD_DIM:(g_kv + 1) * HEAD_DIM] = (
                qs[:, h * HEAD_DIM:(h + 1) * HEAD_DIM])
        kx[s, 0:WINDOW, :] = kc
        kx[s, WINDOW:n_keys, :] = kn
        vx[s, 0:WINDOW, :] = vc
        vx[s, WINDOW:n_keys, :] = vn
        ko_ref[s, 0:WINDOW - tdec, :] = kc[tdec:, :]
        ko_ref[s, WINDOW - tdec:WINDOW, :] = kn
        vo_ref[s, 0:WINDOW - tdec, :] = vc[tdec:, :]
        vo_ref[s, WINDOW - tdec:WINDOW, :] = vn
        return _dot_nt(qbd[s].astype(BF16), kx[s].astype(BF16))

    def stage_pv(s, p):
        r0 = s * tdec
        o_full = _dot(p, vx[s].astype(BF16))
        for h in range(N_HEADS):
            g_kv = h // GQA_GROUP
            o_ref[r0:r0 + tdec, h * HEAD_DIM:(h + 1) * HEAD_DIM] = (
                o_full[h * tdec:(h + 1) * tdec, g_kv * HEAD_DIM:(g_kv + 1) * HEAD_DIM])

    raw = [stage_qk(s) for s in range(sb)]
    probs = [_softmax_with_sink(r + bias_ref[...], sink_ref[...]).astype(BF16) for r in raw]
    for s in range(sb):
        stage_pv(s, probs[s])


def _sample_attn(layer, q, kn, vn, cache_k, cache_v, bias_s, sink_col, sb, tdec):
    depth = cache_k.shape[0]
    rows_all = q.shape[0]
    n_seq = rows_all // tdec
    rows = sb * tdec
    key_pad = 2 * WINDOW
    row_spec = lambda w: pl.BlockSpec((rows, w), lambda i: (i, 0))
    cache_spec = pl.BlockSpec((None, sb, WINDOW, KV_DIM), lambda i: (layer, i, 0, 0))
    in_specs = [
        row_spec(D_MODEL), row_spec(KV_DIM), row_spec(KV_DIM), cache_spec, cache_spec,
        _const_spec((N_HEADS * tdec, key_pad), (0, 0)),
        _const_spec((None, N_HEADS * tdec, 1), (layer, 0, 0)),
    ]
    args = [q, kn, vn, cache_k, cache_v, bias_s, sink_col]
    aliases = {3: 1, 4: 2}
    stacked = jax.ShapeDtypeStruct((depth, n_seq, WINDOW, KV_DIM), F32)
    return pl.pallas_call(
        functools.partial(_sample_attn_body, sb=sb, tdec=tdec),
        grid=(n_seq // sb,),
        in_specs=in_specs,
        out_specs=[row_spec(D_MODEL), cache_spec, cache_spec],
        out_shape=[jax.ShapeDtypeStruct((rows_all, D_MODEL), F32), stacked, stacked],
        input_output_aliases=aliases,
        scratch_shapes=[
            pltpu.VMEM((sb, N_HEADS * tdec, KV_DIM), F32),
            pltpu.VMEM((sb, key_pad, KV_DIM), F32),
            pltpu.VMEM((sb, key_pad, KV_DIM), F32),
        ],
        compiler_params=pltpu.CompilerParams(
            dimension_semantics=("arbitrary",), vmem_limit_bytes=VMEM_LIMIT),
        name=f"sample_attn_l{layer}",
    )(*args)


def _sample_out_body(h_ref, mrg_ref, g2_ref, o_ref, wao_ref, wo_ref, g_ref, wup_ref, wdn_ref,
                     gf_ref, out_ref, *, final_norm):
    yc = _dot(o_ref[...].astype(BF16), wao_ref[...])
    merged = mrg_ref[...] + g2_ref[...] * yc
    h1 = h_ref[...] + _dot(merged.astype(BF16), wo_ref[...])
    y = _mlp_rows(h1, g_ref, wup_ref, wdn_ref)
    if final_norm:
        y = _rms(y, gf_ref[...])
    out_ref[...] = y


def _sample_out(layer, hs, mrg, g2, o, tm, w_attn_out, w_out, norm_mlp, w_up, w_down,
                norm_final, final_norm):
    rows = hs.shape[0]
    lsel3 = (layer, 0, 0)
    row_spec = pl.BlockSpec((tm, D_MODEL), lambda i: (i, 0))
    return pl.pallas_call(
        functools.partial(_sample_out_body, final_norm=final_norm),
        grid=(rows // tm,),
        in_specs=[
            row_spec, row_spec, row_spec, row_spec,
            _const_spec((None, D_MODEL, D_MODEL), lsel3),
            _const_spec((None, D_MODEL, D_MODEL), lsel3),
            _const_spec((None, 1, D_MODEL), lsel3),
            _const_spec((None, D_MODEL, D_FF), lsel3),
            _const_spec((None, D_FF, D_MODEL), lsel3),
            _const_spec((1, D_MODEL), (0, 0)),
        ],
        out_specs=row_spec,
        out_shape=jax.ShapeDtypeStruct((rows, D_MODEL), F32),
        compiler_params=pltpu.CompilerParams(
            dimension_semantics=("arbitrary",), vmem_limit_bytes=VMEM_LIMIT),
        name=f"sample_out_l{layer}",
    )(hs, mrg, g2, o, w_attn_out, w_out, norm_mlp, w_up, w_down, norm_final)


def _forward(x_prompt, x_sample, state_pool, state_conv, cache_k, cache_v, meta_tokens,
             rel_bias, norm_mix, w_in, b_in, pool_w, pool_scale, conv_w, conv_b, conv_norm_g,
             conv_norm_b, w_conv_out, attn_sinks, w_attn_out, w_out, norm_mlp, w_up, w_down,
             norm_final, *, past_len, tm_prompt, tm_sample, sb_in, sb_attn):
    depth = w_in.shape[0]
    nb, seq, _ = x_prompt.shape
    n_seq, tdec, _ = x_sample.shape
    assert past_len >= POOL_HIST and past_len >= WINDOW
    t_valid = seq + N_META
    tp = -(-t_valid // tm_prompt) * tm_prompt
    hp = None
    hs = x_sample.reshape(n_seq * tdec, D_MODEL)

    row3 = lambda a: a.reshape(depth, 1, a.shape[-1])
    w_in_b, pool_w_b, wco_b = w_in.astype(BF16), pool_w.astype(BF16), w_conv_out.astype(BF16)
    wao_b, wo_b = w_attn_out.astype(BF16), w_out.astype(BF16)
    w_up_b, w_down_b = w_up.astype(BF16), w_down.astype(BF16)
    nm3, bin3, ps3 = row3(norm_mix), row3(b_in), row3(pool_scale)
    cb3, cng3, cnb3, nmlp3 = row3(conv_b), row3(conv_norm_g), row3(conv_norm_b), row3(norm_mlp)
    nf2 = norm_final.reshape(1, D_MODEL)

    band = _banded_bias(rel_bias)
    own_only = jnp.asarray(np.arange(2 * WINDOW) >= WINDOW)
    bias_tab = jnp.stack([band, jnp.where(own_only, band, -jnp.inf)])
    bias_tab = bias_tab.reshape(2, N_KV_HEADS, GQA_GROUP, WINDOW, 2 * WINDOW)
    bias_tab = bias_tab.transpose(0, 1, 4, 2, 3).reshape(
        2, N_KV_HEADS, 2 * WINDOW, GQA_GROUP * WINDOW)
    bias_s = band[:, :tdec, :].reshape(N_HEADS * tdec, 2 * WINDOW)
    sink_col = jnp.repeat(attn_sinks.astype(F32), tdec, axis=1).reshape(depth, N_HEADS * tdec, 1)

    cache_k2 = cache_k.reshape(depth, n_seq, WINDOW, KV_DIM)
    cache_v2 = cache_v.reshape(depth, n_seq, WINDOW, KV_DIM)
    sinks = attn_sinks.astype(F32)

    pool_p, conv_p, k_p, v_p = [], [], [], []
    pool_s, conv_s, k_s, v_s = state_pool, state_conv, cache_k2, cache_v2
    for l in range(depth):
        last = l == depth - 1
        hp, pst, cst, kst, vst = _prompt_mixer(
            l, hp, (x_prompt, meta_tokens.astype(F32)), t_valid, tm_prompt, sinks, nm3, w_in_b,
            bin3, pool_w_b, ps3, conv_w, cb3, cng3, cnb3, wco_b, bias_tab, wao_b, wo_b)
        if last:
            y_prompt = _mlp_final(l, hp, seq, tm_prompt, nmlp3, w_up_b, w_down_b, nf2)
        else:
            hp = _mlp(l, hp.reshape(nb * tp, D_MODEL), tm_prompt, nmlp3, w_up_b, w_down_b, nf2,
                      False).reshape(nb, tp, D_MODEL)
        pool_p.append(pst[:, POOL_CARRY - POOL_HIST:])
        conv_p.append(cst[:, CONV_CARRY - CONV_HIST:])
        k_p.append(kst.reshape(nb, WINDOW, N_KV_HEADS, HEAD_DIM))
        v_p.append(vst.reshape(nb, WINDOW, N_KV_HEADS, HEAD_DIM))

        mrg, g2, q, kn, vn, pool_s, conv_s = _sample_in(
            l, hs, sb_in, tdec, nm3, w_in_b, bin3, pool_w_b, ps3, conv_w, cb3, cng3, cnb3,
            wco_b, pool_s, conv_s)
        o, k_s, v_s = _sample_attn(l, q, kn, vn, k_s, v_s, bias_s, sink_col, sb_attn, tdec)
        hs = _sample_out(l, hs, mrg, g2, o, tm_sample, wao_b, wo_b, nmlp3, w_up_b, w_down_b,
                         nf2, last)

    y_sample = hs.reshape(n_seq, tdec, D_MODEL)
    kv_shape = (depth, n_seq, WINDOW, N_KV_HEADS, HEAD_DIM)
    return (y_prompt, y_sample, jnp.stack(pool_p), pool_s, jnp.stack(conv_p), conv_s,
            jnp.stack(k_p), k_s.reshape(kv_shape), jnp.stack(v_p), v_s.reshape(kv_shape))


def kernel(x_prompt, x_sample, state_pool, state_conv, cache_k, cache_v, meta_tokens, rel_bias, norm_mix, w_in, b_in, pool_w, pool_scale, conv_w, conv_b, conv_norm_g, conv_norm_b, w_conv_out, attn_sinks, w_attn_out, w_out, norm_mlp, w_up, w_down, norm_final):
    return _forward(
        x_prompt, x_sample, state_pool, state_conv, cache_k, cache_v, meta_tokens, rel_bias,
        norm_mix, w_in, b_in, pool_w, pool_scale, conv_w, conv_b, conv_norm_g, conv_norm_b,
        w_conv_out, attn_sinks, w_attn_out, w_out, norm_mlp, w_up, w_down, norm_final,
        past_len=8192, tm_prompt=640, tm_sample=512, sb_in=32, sb_attn=8)
```
